```python
import math
import jax, jax.numpy as jnp
from jax import lax
import numpy as np

D_MODEL = 2048
BATCH = 1
SEQ = 8192
DEPTH = 1
DEC_BATCH = 4
DEC_SEQ = 4096
PAST_LEN = 128

HEAD_DIM = 64
ATTN_W = D_MODEL // 2
N_HEADS = ATTN_W // (2 * HEAD_DIM)
POOL_WINDOWS = (2, 4, 8, 16)
N_POOL_GROUPS = len(POOL_WINDOWS)
POOL_W = D_MODEL // 2
POOL_GROUP_W = POOL_W // N_POOL_GROUPS
IN_COLS = 3 * ATTN_W + POOL_W + 2 * D_MODEL
N_GROUPS = 4
EXPERTS_PER_GROUP = 8
TOP_K = 2
D_EXPERT = D_MODEL // 4
ROPE_THETA = 10000.0
EPS = 1e-6
Q_BLOCK = 128

kernel_name = 'gated_diffattn_pool_hmoe_encoder'


def rmsnorm(x, g):
    x32 = x.astype(jnp.float32)
    y = x32 * lax.rsqrt(jnp.mean(x32 * x32, axis=-1, keepdims=True) + EPS)
    return (y * g.astype(jnp.float32)).astype(x.dtype)


def rope_tables(seq):
    inv = 1.0 / (ROPE_THETA ** (jnp.arange(0, HEAD_DIM, 2, dtype=jnp.float32) / HEAD_DIM))
    ang = jnp.arange(seq, dtype=jnp.float32)[:, None] * inv[None, :]
    ang = jnp.concatenate([ang, ang], axis=-1)
    return jnp.cos(ang), jnp.sin(ang)


def apply_rope(x, cos, sin):
    x32 = x.astype(jnp.float32)
    half = HEAD_DIM // 2
    rot = jnp.concatenate([-x32[..., half:], x32[..., :half]], axis=-1)
    c = cos[None, :, None, None, :]
    s = sin[None, :, None, None, :]
    return (x32 * c + rot * s).astype(x.dtype)


def diff_attention(q, k, v, lam):
    b, s = q.shape[0], q.shape[1]
    nb = s // Q_BLOCK
    scale = HEAD_DIM ** -0.5
    qb = q.transpose(0, 2, 3, 1, 4).reshape(b, N_HEADS, 2, nb, Q_BLOCK, HEAD_DIM)
    qb = qb.transpose(3, 0, 1, 2, 4, 5)
    kt = k.transpose(0, 2, 3, 1, 4)
    vt = v.transpose(0, 2, 1, 3)

    def block(qblk):
        sc = jnp.einsum('bhcqd,bhckd->bhcqk', qblk, kt,
                        preferred_element_type=jnp.float32) * scale
        p = jax.nn.softmax(sc, axis=-1)
        a = p[:, :, 0] - lam * p[:, :, 1]
        return jnp.einsum('bhqk,bhkv->bhqv', a.astype(vt.dtype), vt)

    o = lax.map(block, qb)
    return o.transpose(1, 0, 3, 2, 4).reshape(b, s, N_HEADS, 2 * HEAD_DIM)


def multiscale_pool(p, w_grp, scale):
    b, s, _ = p.shape
    t = jnp.arange(s)
    outs = []
    for gi, w in enumerate(POOL_WINDOWS):
        x32 = p[..., gi * POOL_GROUP_W:(gi + 1) * POOL_GROUP_W].astype(jnp.float32)
        cs = jnp.concatenate([jnp.zeros((b, 1, POOL_GROUP_W), jnp.float32),
                              jnp.cumsum(x32, axis=1)], axis=1)
        lo = jnp.clip(t - w // 2, 0, s - 1)
        hi = jnp.clip(t + (w - 1 - w // 2), 0, s - 1)
        cnt = (hi - lo + 1).astype(jnp.float32)
        mean = (jnp.take(cs, hi + 1, axis=1) - jnp.take(cs, lo, axis=1)) / cnt[None, :, None]
        y = (mean - x32).astype(p.dtype)
        outs.append(jnp.einsum('bsc,ce->bse', y, w_grp[gi]))
    return jnp.concatenate(outs, axis=-1) * scale


def hier_moe(x, w_gr, b_gr, w_er, b_er, w_gate, w_up, w_down):
    b, s, d = x.shape
    xt = x.reshape(b * s, d)
    g_logits = jnp.einsum('td,dg->tg', xt, w_gr,
                          preferred_element_type=jnp.float32) + b_gr.astype(jnp.float32)
    g_prob = jax.nn.softmax(g_logits, axis=-1)
    g_sel = jnp.argmax(g_logits, axis=-1)
    g_w = jnp.max(g_prob, axis=-1)
    e_logits = jnp.einsum('td,de->te', xt, w_er,
                          preferred_element_type=jnp.float32) + b_er.astype(jnp.float32)
    e_logits = e_logits.reshape(-1, N_GROUPS, EXPERTS_PER_GROUP)
    e_logits = jnp.take_along_axis(e_logits, g_sel[:, None, None], axis=1)[:, 0]
    e_val, e_idx = lax.top_k(e_logits, TOP_K)
    e_w = jax.nn.softmax(e_val, axis=-1)
    within = jnp.sum(jax.nn.one_hot(e_idx, EXPERTS_PER_GROUP, dtype=jnp.float32)
                     * e_w[..., None], axis=1)
    combine = (jax.nn.one_hot(g_sel, N_GROUPS, dtype=jnp.float32)[:, :, None]
               * (g_w[:, None] * within)[:, None, :]).astype(x.dtype)
    out = jnp.zeros_like(xt)
    for gi in range(N_GROUPS):
        h = jax.nn.silu(jnp.einsum('td,edf->tef', xt, w_gate[gi])) * \
            jnp.einsum('td,edf->tef', xt, w_up[gi])
        out = out + jnp.einsum('tef,efd->td', h * combine[:, gi, :, None], w_down[gi])
    return out.reshape(b, s, d)


def encoder_layer(x, layer_idx, attn_norm_g, w_in, q_norm_g, k_norm_g, lambda_q1, lambda_k1,
                  lambda_q2, lambda_k2, subln_g, w_attn_up, w_pool_grp, pool_scale, w_pool_up,
                  w_out, ffn_norm_g, w_group_router, b_group_router, w_expert_router,
                  b_expert_router, w_gate, w_up, w_down):
    b, s, _ = x.shape
    lam_init = 0.8 - 0.6 * math.exp(-0.3 * layer_idx)
    xn = rmsnorm(x, attn_norm_g)
    proj = jnp.einsum('bsd,dc->bsc', xn, w_in)
    c1 = ATTN_W
    c2 = 2 * ATTN_W
    c3 = 3 * ATTN_W
    c4 = c3 + POOL_W
    c5 = c4 + D_MODEL
    q, k, v, pin, ga, gp = jnp.split(proj, [c1, c2, c3, c4, c5], axis=-1)
    q = rmsnorm(q.reshape(b, s, N_HEADS, 2, HEAD_DIM), q_norm_g)
    k = rmsnorm(k.reshape(b, s, N_HEADS, 2, HEAD_DIM), k_norm_g)
    cos, sin = rope_tables(s)
    q = apply_rope(q, cos, sin)
    k = apply_rope(k, cos, sin)
    v = v.reshape(b, s, N_HEADS, 2 * HEAD_DIM)
    lam = (jnp.exp(jnp.sum(lambda_q1.astype(jnp.float32) * lambda_k1.astype(jnp.float32)))
           - jnp.exp(jnp.sum(lambda_q2.astype(jnp.float32) * lambda_k2.astype(jnp.float32)))
           + lam_init)
    o = diff_attention(q, k, v, lam)
    o = rmsnorm(o, subln_g) * (1.0 - lam_init)
    attn_d = jnp.einsum('bsc,cd->bsd', o.reshape(b, s, ATTN_W), w_attn_up)
    pool_d = jnp.einsum('bsc,cd->bsd', multiscale_pool(pin, w_pool_grp, pool_scale), w_pool_up)
    merged = jax.nn.sigmoid(ga) * attn_d + jax.nn.sigmoid(gp) * pool_d
    x = x + jnp.einsum('bsd,de->bse', merged, w_out)
    x = x + hier_moe(rmsnorm(x, ffn_norm_g), w_group_router, b_group_router, w_expert_router,
                     b_expert_router, w_gate, w_up, w_down)
    return x


def setup_inputs(seed: int = 0) -> dict:
    key = jax.random.key(seed)
    ks = jax.random.split(key, 24)

    def nrm(k, shape, scale):
        return jax.random.normal(k, shape, jnp.float32) * scale

    def gain(k, shape):
        return 1.0 + 0.1 * jax.random.normal(k, shape, jnp.float32)

    G, E, F = N_GROUPS, EXPERTS_PER_GROUP, D_EXPERT
    return {
        'x_prompt': nrm(ks[0], (BATCH, SEQ, D_MODEL), 1.0),
        'x_sample': nrm(ks[1], (DEC_BATCH, DEC_SEQ, D_MODEL), 1.0),
        'attn_norm_g': gain(ks[2], (DEPTH, D_MODEL)),
        'w_in': nrm(ks[3], (DEPTH, D_MODEL, IN_COLS), D_MODEL ** -0.5),
        'q_norm_g': gain(ks[4], (DEPTH, HEAD_DIM)),
        'k_norm_g': gain(ks[5], (DEPTH, HEAD_DIM)),
        'lambda_q1': nrm(ks[6], (DEPTH, HEAD_DIM), 0.1),
        'lambda_k1': nrm(ks[7], (DEPTH, HEAD_DIM), 0.1),
        'lambda_q2': nrm(ks[8], (DEPTH, HEAD_DIM), 0.1),
        'lambda_k2': nrm(ks[9], (DEPTH, HEAD_DIM), 0.1),
        'subln_g': gain(ks[10], (DEPTH, 2 * HEAD_DIM)),
        'w_attn_up': nrm(ks[11], (DEPTH, ATTN_W, D_MODEL), ATTN_W ** -0.5),
        'w_pool_grp': nrm(ks[12], (DEPTH, N_POOL_GROUPS, POOL_GROUP_W, POOL_GROUP_W), POOL_GROUP_W ** -0.5),
        'pool_scale': gain(ks[13], (DEPTH, POOL_W)),
        'w_pool_up': nrm(ks[14], (DEPTH, POOL_W, D_MODEL), POOL_W ** -0.5),
        'w_out': nrm(ks[15], (DEPTH, D_MODEL, D_MODEL), D_MODEL ** -0.5),
        'ffn_norm_g': gain(ks[16], (DEPTH, D_MODEL)),
        'w_group_router': nrm(ks[17], (DEPTH, D_MODEL, G), D_MODEL ** -0.5),
        'b_group_router': nrm(ks[18], (DEPTH, G), 0.01),
        'w_expert_router': nrm(ks[19], (DEPTH, D_MODEL, G * E), D_MODEL ** -0.5),
        'b_expert_router': nrm(ks[20], (DEPTH, G * E), 0.01),
        'w_gate': nrm(ks[21], (DEPTH, G, E, D_MODEL, F), D_MODEL ** -0.5),
        'w_up': nrm(ks[22], (DEPTH, G, E, D_MODEL, F), D_MODEL ** -0.5),
        'w_down': nrm(ks[23], (DEPTH, G, E, F, D_MODEL), F ** -0.5),
    }


def reference(x_prompt, x_sample, attn_norm_g, w_in, q_norm_g, k_norm_g, lambda_q1, lambda_k1,
              lambda_q2, lambda_k2, subln_g, w_attn_up, w_pool_grp, pool_scale, w_pool_up,
              w_out, ffn_norm_g, w_group_router, b_group_router, w_expert_router,
              b_expert_router, w_gate, w_up, w_down):
    y_prompt = x_prompt
    y_sample = x_sample
    for l in range(DEPTH):
        params = (attn_norm_g[l], w_in[l], q_norm_g[l], k_norm_g[l], lambda_q1[l], lambda_k1[l],
                  lambda_q2[l], lambda_k2[l], subln_g[l], w_attn_up[l], w_pool_grp[l],
                  pool_scale[l], w_pool_up[l], w_out[l], ffn_norm_g[l], w_group_router[l],
                  b_group_router[l], w_expert_router[l], b_expert_router[l], w_gate[l],
                  w_up[l], w_down[l])
        y_prompt = encoder_layer(y_prompt, l, *params)
        y_sample = encoder_layer(y_sample, l, *params)
    return (y_prompt, y_sample)
```

```python
import functools
import math

import numpy as np
import jax
import jax.numpy as jnp
from jax import lax
from jax.experimental import pallas as pl
from jax.experimental.pallas import tpu as pltpu

F32 = jnp.float32
BF16 = jnp.bfloat16
I32 = jnp.int32

D_MODEL = 2048
HEAD_DIM = 64
N_HEADS = 8
HEAD_W = 2 * HEAD_DIM
ATTN_W = N_HEADS * HEAD_W
POOL_W = 1024
POOL_WINDOWS = (2, 4, 8, 16)
POOL_GROUP_W = POOL_W // len(POOL_WINDOWS)
POOL_HALO = 64
IN_COLS = 3 * ATTN_W + POOL_W + 2 * D_MODEL
N_GROUPS = 4
EXPERTS_PER_GROUP = 8
N_EXPERTS = N_GROUPS * EXPERTS_PER_GROUP
D_EXPERT = 512
ROPE_THETA = 10000.0
EPS = 1e-6
LANES = 128
ROUTER_COLS = 128
EXPERT_COL0 = 8

COL_Q, COL_K, COL_V, COL_PIN, COL_GA, COL_GP = 0, 1, 2, 3, 4, 6

VMEM_LIMIT = 56 * 1024 * 1024


def _cparams(sem, vmem=VMEM_LIMIT):
    return pltpu.CompilerParams(dimension_semantics=sem, vmem_limit_bytes=vmem)


def _inproj_kernel(posblk_ref, x0_ref, x1_ref, gattn_ref, w_ref, cos_ref, sin_ref, qkg_ref, bd_ref,
                   o_ref, xn_ref, *, nt0, rc):
    del posblk_ref
    i = pl.program_id(0)
    j = pl.program_id(1)
    tm = xn_ref.shape[0]
    n_chunks = tm // rc

    def normalise(x_ref):
        for c in range(n_chunks):
            xs = x_ref[c * rc:(c + 1) * rc, :]
            ms = jnp.mean(xs * xs, axis=-1, keepdims=True)
            xn_ref[c * rc:(c + 1) * rc, :] = (xs * lax.rsqrt(ms + EPS) * gattn_ref[...]).astype(BF16)

    @pl.when((j == 0) & (i < nt0))
    def _():
        normalise(x0_ref)

    @pl.when((j == 0) & (i >= nt0))
    def _():
        normalise(x1_ref)

    def chunk_acc(c):
        return jnp.dot(xn_ref[c * rc:(c + 1) * rc, :], w_ref[...], preferred_element_type=F32)

    @pl.when(j <= COL_K)
    def _():
        g = jnp.where(j == COL_Q, qkg_ref[0:1, :], qkg_ref[1:2, :])
        lane = lax.broadcasted_iota(I32, (rc, LANES), 1)
        first_half = (lane % HEAD_DIM) < (HEAD_DIM // 2)
        for c in range(n_chunks):
            a = chunk_acc(c)
            cos = cos_ref[c * rc:(c + 1) * rc, :]
            sin = sin_ref[c * rc:(c + 1) * rc, :]
            for t in range(a.shape[1] // LANES):
                at = a[:, t * LANES:(t + 1) * LANES]
                ms = jnp.dot((at * at).astype(BF16), bd_ref[...], preferred_element_type=F32)
                u = at * g[:, t * LANES:(t + 1) * LANES]
                rot = jnp.where(first_half, pltpu.roll(u, LANES - HEAD_DIM // 2, 1),
                                pltpu.roll(u, HEAD_DIM // 2, 1))
                val = (u * cos + rot * sin) * lax.rsqrt(ms + EPS)
                o_ref[c * rc:(c + 1) * rc, t * LANES:(t + 1) * LANES] = val.astype(BF16)

    @pl.when((j == COL_V) | (j == COL_PIN))
    def _():
        for c in range(n_chunks):
            o_ref[c * rc:(c + 1) * rc, :] = chunk_acc(c).astype(BF16)

    @pl.when(j >= COL_GA)
    def _():
        for c in range(n_chunks):
            o_ref[c * rc:(c + 1) * rc, :] = jax.nn.sigmoid(chunk_acc(c)).astype(BF16)


def _in_projection(x0, x1, posblk, gattn, w_in, cos_t, sin_t, qkg, bd, *, tm, tn, rc):
    t0, t1 = x0.shape[0], x1.shape[0]
    nt0, nt1 = t0 // tm, t1 // tm
    nt = nt0 + nt1
    grid_spec = pltpu.PrefetchScalarGridSpec(
        num_scalar_prefetch=1,
        grid=(nt, IN_COLS // tn),
        in_specs=[
            pl.BlockSpec((tm, D_MODEL), lambda i, j, pb: (jnp.minimum(i, nt0 - 1), 0)),
            pl.BlockSpec((tm, D_MODEL), lambda i, j, pb: (jnp.maximum(i - nt0, 0), 0)),
            pl.BlockSpec((1, D_MODEL), lambda i, j, pb: (0, 0)),
            pl.BlockSpec((D_MODEL, tn), lambda i, j, pb: (0, j)),
            pl.BlockSpec((tm, LANES), lambda i, j, pb: (pb[i], 0)),
            pl.BlockSpec((tm, LANES), lambda i, j, pb: (pb[i], 0)),
            pl.BlockSpec((8, tn), lambda i, j, pb: (0, 0)),
            pl.BlockSpec((LANES, LANES), lambda i, j, pb: (0, 0)),
        ],
        out_specs=pl.BlockSpec((tm, tn), lambda i, j, pb: (i, j)),
        scratch_shapes=[pltpu.VMEM((tm, D_MODEL), BF16)],
    )
    return pl.pallas_call(
        functools.partial(_inproj_kernel, nt0=nt0, rc=rc),
        grid_spec=grid_spec,
        out_shape=jax.ShapeDtypeStruct((t0 + t1, IN_COLS), BF16),
        compiler_params=_cparams(("arbitrary", "arbitrary")),
        name="in_projection",
    )(posblk, x0, x1, gattn, w_in, cos_t, sin_t, qkg, bd)


def _attn_kernel(q_ref, k_ref, v_ref, lam_ref, sg_ref, o_ref, vt_ref, acc_ref, m_ref, l_ref,
                 *, tk, lam_init):
    i = pl.program_id(2)
    s_len = k_ref.shape[0]
    tq = q_ref.shape[0]
    nk = s_len // tk

    @pl.when(i == 0)
    def _():
        for jb in range(nk):
            vt_ref[:, jb * tk:(jb + 1) * tk] = v_ref[jb * tk:(jb + 1) * tk, :].astype(F32).T.astype(BF16)

    q = q_ref[...]
    lane = lax.broadcasted_iota(I32, q.shape, 1)
    zero = jnp.zeros_like(q)
    qc = (jnp.where(lane < HEAD_DIM, q, zero), jnp.where(lane >= HEAD_DIM, q, zero))

    m_ref[...] = jnp.full(m_ref.shape, -jnp.inf, F32)
    l_ref[...] = jnp.zeros(l_ref.shape, F32)
    acc_ref[...] = jnp.zeros(acc_ref.shape, F32)

    def step(jb, carry):
        off = pl.multiple_of(jb * tk, tk)
        kb = k_ref[pl.ds(off, tk), :]
        vtb = vt_ref[:, pl.ds(off, tk)]
        for c in range(2):
            st = lax.dot_general(kb, qc[c], (((1,), (1,)), ((), ())), preferred_element_type=F32)
            m_old = m_ref[c]
            m_new = jnp.maximum(m_old, jnp.max(st, axis=0, keepdims=True))
            alpha = jnp.exp2(m_old - m_new)
            p = jnp.exp2(st - m_new)
            l_ref[c] = alpha * l_ref[c] + jnp.sum(p, axis=0, keepdims=True)
            acc_ref[c] = alpha * acc_ref[c] + jnp.dot(vtb, p.astype(BF16), preferred_element_type=F32)
            m_ref[c] = m_new
        return carry

    lax.fori_loop(0, nk, step, 0)

    lq1, lk1, lq2, lk2 = lam_ref[0:1, :], lam_ref[1:2, :], lam_ref[2:3, :], lam_ref[3:4, :]
    lam = (jnp.exp(jnp.sum(lq1 * lk1, axis=-1, keepdims=True))
           - jnp.exp(jnp.sum(lq2 * lk2, axis=-1, keepdims=True)) + lam_init)
    ot = acc_ref[0] / l_ref[0] - lam * (acc_ref[1] / l_ref[1])
    ms = jnp.mean(ot * ot, axis=0, keepdims=True)
    on = (ot * lax.rsqrt(ms + EPS)).T
    o_ref[...] = (on * sg_ref[...] * (1.0 - lam_init)).astype(BF16)


def _attention(proj, tok_off, batch, s_len, lam_pack, subln_g, *, tq, tk, lam_init):
    qb0 = tok_off // tq
    sb0 = tok_off // s_len
    nq = s_len // tq
    return pl.pallas_call(
        functools.partial(_attn_kernel, tk=tk, lam_init=lam_init),
        grid=(batch, N_HEADS, nq),
        in_specs=[
            pl.BlockSpec((tq, HEAD_W), lambda b, h, i: (qb0 + b * nq + i, COL_Q * N_HEADS + h)),
            pl.BlockSpec((s_len, HEAD_W), lambda b, h, i: (sb0 + b, COL_K * N_HEADS + h)),
            pl.BlockSpec((s_len, HEAD_W), lambda b, h, i: (sb0 + b, COL_V * N_HEADS + h)),
            pl.BlockSpec((8, HEAD_DIM), lambda b, h, i: (0, 0)),
            pl.BlockSpec((1, HEAD_W), lambda b, h, i: (0, 0)),
        ],
        out_specs=pl.BlockSpec((tq, HEAD_W), lambda b, h, i: (b * nq + i, h)),
        out_shape=jax.ShapeDtypeStruct((batch * s_len, ATTN_W), BF16),
        scratch_shapes=[
            pltpu.VMEM((HEAD_W, s_len), BF16),
            pltpu.VMEM((2, HEAD_W, tq), F32),
            pltpu.VMEM((2, 1, tq), F32),
            pltpu.VMEM((2, 1, tq), F32),
        ],
        compiler_params=_cparams(("arbitrary", "arbitrary", "arbitrary")),
        name="diff_attention",
    )(proj, proj, proj, lam_pack, subln_g)


def _post_kernel(pos_ref, len_ref, x0_ref, x1_ref, o0_ref, o1_ref, pin_ref, prev_ref, next_ref,
                 ga_ref, gp_ref, wau_ref, wgrp_ref, pscale_ref, wpu_ref, wout_ref, gffn_ref,
                 wrh_ref, wrl_ref, br_ref, xnew_ref, ids_ref, wts_ref, *, nt0):
    i = pl.program_id(0)
    tm = pin_ref.shape[0]
    p0 = pos_ref[i]
    s_len = len_ref[i]
    first = i < nt0

    o = jnp.where(first, o0_ref[...], o1_ref[...])
    attn_d = jnp.dot(o, wau_ref[...], preferred_element_type=F32)

    ext = jnp.concatenate([prev_ref[...], pin_ref[...], next_ref[...]], axis=0)
    ke = tm + 2 * POOL_HALO
    r_band = lax.broadcasted_iota(I32, (tm, ke), 0) + p0
    c_band = lax.broadcasted_iota(I32, (tm, ke), 1) + (p0 - POOL_HALO)
    r_cnt = lax.broadcasted_iota(I32, (tm, POOL_GROUP_W), 0) + p0
    pooled = []
    for gi, w in enumerate(POOL_WINDOWS):
        back, fwd = w // 2, w - 1 - w // 2
        lo = jnp.maximum(r_band - back, 0)
        hi = jnp.minimum(r_band + fwd, s_len - 1)
        band = jnp.where((c_band >= lo) & (c_band <= hi), 1.0, 0.0).astype(BF16)
        cnt = jnp.minimum(r_cnt + fwd, s_len - 1) - jnp.maximum(r_cnt - back, 0) + 1
        cols = slice(gi * POOL_GROUP_W, (gi + 1) * POOL_GROUP_W)
        wsum = jnp.dot(band, ext[:, cols], preferred_element_type=F32)
        y = wsum / cnt.astype(F32) - pin_ref[:, cols].astype(F32)
        pooled.append(jnp.dot(y.astype(BF16), wgrp_ref[gi], preferred_element_type=F32))
    pooled = (jnp.concatenate(pooled, axis=1) * pscale_ref[...]).astype(BF16)
    pool_d = jnp.dot(pooled, wpu_ref[...], preferred_element_type=F32)

    merged = ga_ref[...].astype(F32) * attn_d + gp_ref[...].astype(F32) * pool_d
    x = jnp.where(first, x0_ref[...], x1_ref[...])
    xnew = x + jnp.dot(merged.astype(BF16), wout_ref[...], preferred_element_type=F32)
    xnew_ref[...] = xnew

    ms = jnp.mean(xnew * xnew, axis=-1, keepdims=True)
    hn = xnew * lax.rsqrt(ms + EPS) * gffn_ref[...]
    hi_part = hn.astype(BF16)
    lo_part = (hn - hi_part.astype(F32)).astype(BF16)
    logits = (jnp.dot(hi_part, wrh_ref[...], preferred_element_type=F32)
              + (jnp.dot(hi_part, wrl_ref[...], preferred_element_type=F32)
                 + jnp.dot(lo_part, wrh_ref[...], preferred_element_type=F32))
              + br_ref[...])
    lt = logits.T

    gl = lt[0:N_GROUPS, :]
    g_iota = lax.broadcasted_iota(I32, gl.shape, 0)
    g_max = jnp.max(gl, axis=0, keepdims=True)
    g_sel = jnp.min(jnp.where(gl == g_max, g_iota, N_GROUPS), axis=0, keepdims=True)
    g_w = 1.0 / jnp.sum(jnp.exp(gl - g_max), axis=0, keepdims=True)
    el = lt[EXPERT_COL0:EXPERT_COL0 + N_EXPERTS, :]
    e_iota = lax.broadcasted_iota(I32, el.shape, 0)
    neg = jnp.float32(-jnp.inf)
    el = jnp.where(e_iota // EXPERTS_PER_GROUP == g_sel, el, neg)
    v1 = jnp.max(el, axis=0, keepdims=True)
    i1 = jnp.min(jnp.where(el == v1, e_iota, N_EXPERTS), axis=0, keepdims=True)
    el2 = jnp.where(e_iota == i1, neg, el)
    v2 = jnp.max(el2, axis=0, keepdims=True)
    i2 = jnp.min(jnp.where(el2 == v2, e_iota, N_EXPERTS), axis=0, keepdims=True)
    d = jnp.exp(v2 - v1)
    w1 = g_w / (1.0 + d)
    w2 = g_w * d / (1.0 + d)
    row = lax.broadcasted_iota(I32, (8, tm), 0)
    ids_ref[...] = jnp.where(row == 0, i1, jnp.where(row == 1, i2, 0))
    wts_ref[...] = jnp.where(row == 0, w1, jnp.where(row == 1, w2, 0.0))


def _post_block(x0, x1, o0, o1, proj, tile_pos, tile_len, wau, wgrp, pscale, wpu, wout, gffn,
                wrh, wrl, br, *, tm):
    t0, t1 = x0.shape[0], x1.shape[0]
    t = t0 + t1
    nt0 = t0 // tm
    nt = t // tm
    hb = tm // POOL_HALO
    n_hblk = t // POOL_HALO
    const = lambda *shape: pl.BlockSpec(shape, lambda i, tp, tl: (0,) * len(shape),
                                        pipeline_mode=pl.Buffered(1))
    grid_spec = pltpu.PrefetchScalarGridSpec(
        num_scalar_prefetch=2,
        grid=(nt,),
        in_specs=[
            pl.BlockSpec((tm, D_MODEL), lambda i, tp, tl: (jnp.minimum(i, nt0 - 1), 0)),
            pl.BlockSpec((tm, D_MODEL), lambda i, tp, tl: (jnp.maximum(i - nt0, 0), 0)),
            pl.BlockSpec((tm, ATTN_W), lambda i, tp, tl: (jnp.minimum(i, nt0 - 1), 0)),
            pl.BlockSpec((tm, ATTN_W), lambda i, tp, tl: (jnp.maximum(i - nt0, 0), 0)),
            pl.BlockSpec((tm, POOL_W), lambda i, tp, tl: (i, COL_PIN)),
            pl.BlockSpec((POOL_HALO, POOL_W), lambda i, tp, tl: (jnp.maximum(i * hb - 1, 0), COL_PIN)),
            pl.BlockSpec((POOL_HALO, POOL_W), lambda i, tp, tl: (jnp.minimum((i + 1) * hb, n_hblk - 1), COL_PIN)),
            pl.BlockSpec((tm, D_MODEL), lambda i, tp, tl: (i, COL_GA // 2)),
            pl.BlockSpec((tm, D_MODEL), lambda i, tp, tl: (i, COL_GP // 2)),
            const(ATTN_W, D_MODEL),
            const(len(POOL_WINDOWS), POOL_GROUP_W, POOL_GROUP_W),
            const(1, POOL_W),
            const(POOL_W, D_MODEL),
            const(D_MODEL, D_MODEL),
            const(1, D_MODEL),
            const(D_MODEL, ROUTER_COLS),
            const(D_MODEL, ROUTER_COLS),
            const(1, ROUTER_COLS),
        ],
        out_specs=[
            pl.BlockSpec((tm, D_MODEL), lambda i, tp, tl: (i, 0)),
            pl.BlockSpec((8, tm), lambda i, tp, tl: (0, i)),
            pl.BlockSpec((8, tm), lambda i, tp, tl: (0, i)),
        ],
    )
    return pl.pallas_call(
        functools.partial(_post_kernel, nt0=nt0),
        grid_spec=grid_spec,
        out_shape=[
            jax.ShapeDtypeStruct((t, D_MODEL), F32),
            jax.ShapeDtypeStruct((8, t), I32),
            jax.ShapeDtypeStruct((8, t), F32),
        ],
        compiler_params=_cparams(("arbitrary",)),
        name="post_block",
    )(tile_pos, tile_len, x0, x1, o0, o1, proj, proj, proj, proj, proj, wau, wgrp, pscale, wpu,
      wout, gffn, wrh, wrl, br)


def _row_gather_start(src_hbm, idx_ref, base, n_rows, buf, sem):
    def body(r, carry):
        row = idx_ref[base + r]
        pltpu.make_async_copy(src_hbm.at[pl.ds(row, 1)], buf.at[pl.ds(r, 1)], sem).start()
        return carry
    lax.fori_loop(0, n_rows, body, 0)


def _row_gather_wait(src_hbm, n_rows, buf, sem):
    def body(r, carry):
        pltpu.make_async_copy(src_hbm.at[pl.ds(0, 1)], buf.at[pl.ds(r, 1)], sem).wait()
        return carry
    lax.fori_loop(0, n_rows, body, 0)


def _moe_kernel(te_ref, tf_ref, nu_ref, src_ref, x_hbm, gffn_ref, ws_ref, wg_ref, wu_ref, wd_ref,
                y_ref, xbuf, sem, wgb, wub, wdb):
    del te_ref
    i = pl.program_id(0)
    tm = y_ref.shape[0]
    n_used = nu_ref[0]
    slot = lax.rem(i, 2)

    @pl.when(i == 0)
    def _():
        _row_gather_start(x_hbm, src_ref, 0, tm, xbuf.at[0], sem.at[0])

    @pl.when(i + 1 < n_used)
    def _():
        _row_gather_start(x_hbm, src_ref, (i + 1) * tm, tm, xbuf.at[1 - slot], sem.at[1 - slot])

    @pl.when(i < n_used)
    def _():
        @pl.when(tf_ref[i] == 1)
        def _():
            wgb[...] = wg_ref[0].astype(BF16)
            wub[...] = wu_ref[0].astype(BF16)
            wdb[...] = wd_ref[0].astype(BF16)

        _row_gather_wait(x_hbm, tm, xbuf.at[slot], sem.at[slot])
        xs = xbuf[slot]
        ms = jnp.mean(xs * xs, axis=-1, keepdims=True)
        hn = (xs * lax.rsqrt(ms + EPS) * gffn_ref[...]).astype(BF16)
        gate = jnp.dot(hn, wgb[...], preferred_element_type=F32)
        up = jnp.dot(hn, wub[...], preferred_element_type=F32)
        h = jax.nn.silu(gate) * up * ws_ref[...]
        y_ref[...] = jnp.dot(h.astype(BF16), wdb[...], preferred_element_type=F32)

    @pl.when(i >= n_used)
    def _():
        y_ref[...] = jnp.zeros(y_ref.shape, y_ref.dtype)


def _moe_ffn(xnew, tile_e, tile_first, n_used, src_row, w_slot, gffn, w_gate, w_up, w_down, *, tm):
    p_rows = src_row.shape[0]
    n_tiles = p_rows // tm
    grid_spec = pltpu.PrefetchScalarGridSpec(
        num_scalar_prefetch=4,
        grid=(n_tiles,),
        in_specs=[
            pl.BlockSpec(memory_space=pl.ANY),
            pl.BlockSpec((1, D_MODEL), lambda i, te, tf, nu, sr: (0, 0)),
            pl.BlockSpec((tm, 1), lambda i, te, tf, nu, sr: (i, 0)),
            pl.BlockSpec((1, D_MODEL, D_EXPERT), lambda i, te, tf, nu, sr: (te[i], 0, 0)),
            pl.BlockSpec((1, D_MODEL, D_EXPERT), lambda i, te, tf, nu, sr: (te[i], 0, 0)),
            pl.BlockSpec((1, D_EXPERT, D_MODEL), lambda i, te, tf, nu, sr: (te[i], 0, 0)),
        ],
        out_specs=pl.BlockSpec((tm, D_MODEL), lambda i, te, tf, nu, sr: (i, 0)),
        scratch_shapes=[
            pltpu.VMEM((2, tm, D_MODEL), F32),
            pltpu.SemaphoreType.DMA((2,)),
            pltpu.VMEM((D_MODEL, D_EXPERT), BF16),
            pltpu.VMEM((D_MODEL, D_EXPERT), BF16),
            pltpu.VMEM((D_EXPERT, D_MODEL), BF16),
        ],
    )
    return pl.pallas_call(
        _moe_kernel,
        grid_spec=grid_spec,
        out_shape=jax.ShapeDtypeStruct((p_rows, D_MODEL), F32),
        compiler_params=_cparams(("arbitrary",)),
        name="moe_ffn",
    )(tile_e, tile_first, n_used, src_row, xnew, gffn, w_slot, w_gate, w_up, w_down)


def _combine_kernel(pos_ref, xnew_ref, y_hbm, out0_ref, out1_ref, ybuf, sem, *, nt0, n_tok):
    i = pl.program_id(0)
    nt = pl.num_programs(0)
    tm = xnew_ref.shape[0]
    slot = lax.rem(i, 2)

    def start(tile, s):
        _row_gather_start(y_hbm, pos_ref, tile * tm, tm, ybuf.at[s, 0], sem.at[s])
        _row_gather_start(y_hbm, pos_ref, n_tok + tile * tm, tm, ybuf.at[s, 1], sem.at[s])

    @pl.when(i == 0)
    def _():
        start(0, 0)

    @pl.when(i + 1 < nt)
    def _():
        start(i + 1, 1 - slot)

    _row_gather_wait(y_hbm, tm, ybuf.at[slot, 0], sem.at[slot])
    _row_gather_wait(y_hbm, tm, ybuf.at[slot, 1], sem.at[slot])
    val = xnew_ref[...] + (ybuf[slot, 0] + ybuf[slot, 1])

    @pl.when(i < nt0)
    def _():
        out0_ref[...] = val

    @pl.when(i >= nt0)
    def _():
        out1_ref[...] = val


def _combine(xnew, y_sorted, pos, t0, *, tm):
    t = xnew.shape[0]
    nt0 = t0 // tm
    nt = t // tm
    grid_spec = pltpu.PrefetchScalarGridSpec(
        num_scalar_prefetch=1,
        grid=(nt,),
        in_specs=[
            pl.BlockSpec((tm, D_MODEL), lambda i, ps: (i, 0)),
            pl.BlockSpec(memory_space=pl.ANY),
        ],
        out_specs=[
            pl.BlockSpec((tm, D_MODEL), lambda i, ps: (jnp.minimum(i, nt0 - 1), 0)),
            pl.BlockSpec((tm, D_MODEL), lambda i, ps: (jnp.maximum(i - nt0, 0), 0)),
        ],
        scratch_shapes=[
            pltpu.VMEM((2, 2, tm, D_MODEL), F32),
            pltpu.SemaphoreType.DMA((2,)),
        ],
    )
    return pl.pallas_call(
        functools.partial(_combine_kernel, nt0=nt0, n_tok=t),
        grid_spec=grid_spec,
        out_shape=[
            jax.ShapeDtypeStruct((t0, D_MODEL), F32),
            jax.ShapeDtypeStruct((t - t0, D_MODEL), F32),
        ],
        compiler_params=_cparams(("arbitrary",)),
        name="moe_combine",
    )(pos, xnew, y_sorted)


def _rope_tables(max_len):
    inv = 1.0 / (ROPE_THETA ** (jnp.arange(0, HEAD_DIM, 2, dtype=F32) / HEAD_DIM))
    ang = jnp.arange(max_len, dtype=F32)[:, None] * inv[None, :]
    ang = jnp.concatenate([ang, ang, ang, ang], axis=-1)
    lane = jnp.arange(LANES)
    sign = jnp.where((lane % HEAD_DIM) < HEAD_DIM // 2, -1.0, 1.0).astype(F32)
    return jnp.cos(ang), jnp.sin(ang) * sign[None, :]


def _tile_tables(groups, tm):
    pos, length = [], []
    for batch, s_len in groups:
        for _ in range(batch):
            for p in range(0, s_len, tm):
                pos.append(p)
                length.append(s_len)
    return np.asarray(pos, np.int32), np.asarray(length, np.int32)


def _routing_tables(ids, wts, n_tok, tm):
    e_flat = ids[0:2, :].reshape(-1)
    w_flat = wts[0:2, :].reshape(-1)
    onehot = (e_flat[:, None] == jnp.arange(N_EXPERTS, dtype=I32)[None, :]).astype(I32)
    csum = jnp.cumsum(onehot, axis=0)
    rank = jnp.sum(csum * onehot, axis=1) - 1
    counts = csum[-1]
    padded = ((counts + tm - 1) // tm) * tm
    ends = jnp.cumsum(padded)
    starts = ends - padded
    dst = (starts[e_flat] + rank).astype(I32)
    p_rows = 2 * n_tok + N_EXPERTS * tm
    n_tiles = p_rows // tm
    tok = jnp.tile(jnp.arange(n_tok, dtype=I32), 2)
    src_row = jnp.zeros((p_rows,), I32).at[dst].set(tok)
    w_slot = jnp.zeros((p_rows,), F32).at[dst].set(w_flat)
    n_used = (ends[-1] // tm).astype(I32)
    tile_start = jnp.arange(n_tiles, dtype=I32) * tm
    tile_e = jnp.minimum(jnp.searchsorted(ends, tile_start, side="right"), N_EXPERTS - 1).astype(I32)
    active = jnp.arange(n_tiles, dtype=I32) < n_used
    last_e = tile_e[jnp.maximum(n_used - 1, 0)]
    tile_e = jnp.where(active, tile_e, last_e)
    tile_first = (active & (tile_start == starts[tile_e])).astype(I32)
    return tile_e, tile_first, n_used.reshape(1), src_row, w_slot[:, None], dst


def _forward(x0, x1, groups, params, *, tm_proj, tn_proj, rc_proj, tq, tk, tm_post, tm_moe, tm_comb):
    (attn_norm_g, w_in, q_norm_g, k_norm_g, lambda_q1, lambda_k1, lambda_q2, lambda_k2, subln_g,
     w_attn_up, w_pool_grp, pool_scale, w_pool_up, w_out, ffn_norm_g, w_group_router,
     b_group_router, w_expert_router, b_expert_router, w_gate, w_up, w_down) = params
    t0, t1 = x0.shape[0], x1.shape[0]
    n_tok = t0 + t1
    lam_init = 0.8 - 0.6 * math.exp(-0.3 * 0)
    max_len = max(s for _, s in groups)

    cos_t, sin_t = _rope_tables(max_len)
    pos_proj, _ = _tile_tables(groups, tm_proj)
    q_scale = HEAD_DIM ** -0.5 * math.log2(math.e)
    qkg = jnp.zeros((8, ATTN_W), F32)
    qkg = qkg.at[0].set(jnp.tile(q_norm_g, ATTN_W // HEAD_DIM) * q_scale)
    qkg = qkg.at[1].set(jnp.tile(k_norm_g, ATTN_W // HEAD_DIM))
    seg = np.arange(LANES) // HEAD_DIM
    bd = jnp.asarray((seg[:, None] == seg[None, :]).astype(np.float32) / HEAD_DIM, BF16)

    proj = _in_projection(x0, x1, jnp.asarray(pos_proj // tm_proj), attn_norm_g[None, :],
                          w_in.astype(BF16), cos_t, sin_t, qkg, bd, tm=tm_proj, tn=tn_proj, rc=rc_proj)

    lam_pack = jnp.zeros((8, HEAD_DIM), F32)
    lam_pack = lam_pack.at[0].set(lambda_q1).at[1].set(lambda_k1).at[2].set(lambda_q2).at[3].set(lambda_k2)
    outs = []
    tok_off = 0
    for batch, s_len in groups:
        outs.append(_attention(proj, tok_off, batch, s_len, lam_pack, subln_g[None, :],
                               tq=min(tq, s_len), tk=min(tk, s_len), lam_init=lam_init))
        tok_off += batch * s_len

    tile_pos, tile_len = _tile_tables(groups, tm_post)
    w_router = jnp.zeros((D_MODEL, ROUTER_COLS), F32)
    w_router = w_router.at[:, 0:N_GROUPS].set(w_group_router)
    w_router = w_router.at[:, EXPERT_COL0:EXPERT_COL0 + N_EXPERTS].set(w_expert_router)
    b_router = jnp.zeros((1, ROUTER_COLS), F32)
    b_router = b_router.at[0, 0:N_GROUPS].set(b_group_router)
    b_router = b_router.at[0, EXPERT_COL0:EXPERT_COL0 + N_EXPERTS].set(b_expert_router)
    wr_hi = w_router.astype(BF16)
    wr_lo = (w_router - wr_hi.astype(F32)).astype(BF16)
    xnew, ids, wts = _post_block(
        x0, x1, outs[0], outs[1], proj, jnp.asarray(tile_pos), jnp.asarray(tile_len),
        w_attn_up.astype(BF16), w_pool_grp.astype(BF16), pool_scale[None, :], w_pool_up.astype(BF16),
        w_out.astype(BF16), ffn_norm_g[None, :], wr_hi, wr_lo, b_router, tm=tm_post)

    tile_e, tile_first, n_used, src_row, w_slot, dst = _routing_tables(ids, wts, n_tok, tm_moe)
    y_sorted = _moe_ffn(
        xnew, tile_e, tile_first, n_used, src_row, w_slot, ffn_norm_g[None, :],
        w_gate.reshape(N_EXPERTS, D_MODEL, D_EXPERT), w_up.reshape(N_EXPERTS, D_MODEL, D_EXPERT),
        w_down.reshape(N_EXPERTS, D_EXPERT, D_MODEL), tm=tm_moe)
    return _combine(xnew, y_sorted, dst, t0, tm=tm_comb)


def kernel(x_prompt, x_sample, attn_norm_g, w_in, q_norm_g, k_norm_g, lambda_q1, lambda_k1, lambda_q2,
           lambda_k2, subln_g, w_attn_up, w_pool_grp, pool_scale, w_pool_up, w_out, ffn_norm_g,
           w_group_router, b_group_router, w_expert_router, b_expert_router, w_gate, w_up, w_down):
    assert attn_norm_g.shape[0] == 1, "single-layer stack"
    params = tuple(p[0] for p in (
        attn_norm_g, w_in, q_norm_g, k_norm_g, lambda_q1, lambda_k1, lambda_q2, lambda_k2, subln_g,
        w_attn_up, w_pool_grp, pool_scale, w_pool_up, w_out, ffn_norm_g, w_group_router,
        b_group_router, w_expert_router, b_expert_router, w_gate, w_up, w_down))
    groups = (x_prompt.shape[:2], x_sample.shape[:2])
    y0, y1 = _forward(
        x_prompt.reshape(-1, D_MODEL), x_sample.reshape(-1, D_MODEL), groups, params,
        tm_proj=512, tn_proj=1024, rc_proj=512, tq=512, tk=512, tm_post=256, tm_moe=256, tm_comb=256)
    return y0.reshape(x_prompt.shape), y1.reshape(x_sample.shape)
```

```python
import functools
import math

import numpy as np
import jax
import jax.numpy as jnp
from jax import lax
from jax.experimental import pallas as pl
from jax.experimental.pallas import tpu as pltpu

F32 = jnp.float32
BF16 = jnp.bfloat16
I32 = jnp.int32

D_MODEL = 2048
HEAD_DIM = 64
N_HEADS = 8
HEAD_W = 2 * HEAD_DIM
ATTN_W = N_HEADS * HEAD_W
POOL_W = 1024
POOL_WINDOWS = (2, 4, 8, 16)
POOL_GROUP_W = POOL_W // len(POOL_WINDOWS)
POOL_HALO = 64
IN_COLS = 3 * ATTN_W + POOL_W + 2 * D_MODEL
N_GROUPS = 4
EXPERTS_PER_GROUP = 8
N_EXPERTS = N_GROUPS * EXPERTS_PER_GROUP
D_EXPERT = 512
ROPE_THETA = 10000.0
EPS = 1e-6
LANES = 128
ROUTER_COLS = 128
EXPERT_COL0 = 8
ROW_TILES = D_MODEL // LANES
SAFE_SCORE_BOUND = 50.0
SCORE_BOUND_MARGIN = 1.02

COL_Q, COL_K, COL_V, COL_PIN, COL_GA, COL_GP = 0, 1, 2, 3, 4, 6

VMEM_LIMIT = 56 * 1024 * 1024


def _cparams(sem, vmem=VMEM_LIMIT):
    return pltpu.CompilerParams(dimension_semantics=sem, vmem_limit_bytes=vmem)


def _token_slab(ref, n_tok, s):
    return ref[pl.ds(s, n_tok, stride=ROW_TILES), :]


def _load_token_rows(ref, n_tok):
    return jnp.concatenate([_token_slab(ref, n_tok, s) for s in range(ROW_TILES)], axis=1)


def _store_token_rows(ref, value):
    n_tok = value.shape[0]
    for s in range(ROW_TILES):
        ref[pl.ds(s, n_tok, stride=ROW_TILES), :] = value[:, s * LANES:(s + 1) * LANES]


def _inproj_kernel(posblk_ref, x0_ref, x1_ref, gattn_ref, w_ref, cos_ref, sin_ref, qkg_ref, bd_ref,
                   o_ref, xn_ref, *, nt0, rc):
    del posblk_ref
    i = pl.program_id(0)
    j = pl.program_id(1)
    tm = xn_ref.shape[0]
    n_chunks = tm // rc

    def normalise(x_ref):
        for c in range(n_chunks):
            xs = x_ref[c * rc:(c + 1) * rc, :]
            ms = jnp.mean(xs * xs, axis=-1, keepdims=True)
            xn_ref[c * rc:(c + 1) * rc, :] = (xs * lax.rsqrt(ms + EPS) * gattn_ref[...]).astype(BF16)

    @pl.when((j == 0) & (i < nt0))
    def _():
        normalise(x0_ref)

    @pl.when((j == 0) & (i >= nt0))
    def _():
        normalise(x1_ref)

    def chunk_acc(c):
        return jnp.dot(xn_ref[c * rc:(c + 1) * rc, :], w_ref[...], preferred_element_type=F32)

    @pl.when(j <= COL_K)
    def _():
        g = jnp.where(j == COL_Q, qkg_ref[0:1, :], qkg_ref[1:2, :])
        lane = lax.broadcasted_iota(I32, (rc, LANES), 1)
        first_half = (lane % HEAD_DIM) < (HEAD_DIM // 2)
        for c in range(n_chunks):
            a = chunk_acc(c)
            cos = cos_ref[c * rc:(c + 1) * rc, :]
            sin = sin_ref[c * rc:(c + 1) * rc, :]
            for t in range(a.shape[1] // LANES):
                at = a[:, t * LANES:(t + 1) * LANES]
                ms = jnp.dot((at * at).astype(BF16), bd_ref[...], preferred_element_type=F32)
                u = at * g[:, t * LANES:(t + 1) * LANES]
                rot = jnp.where(first_half, pltpu.roll(u, LANES - HEAD_DIM // 2, 1),
                                pltpu.roll(u, HEAD_DIM // 2, 1))
                val = (u * cos + rot * sin) * lax.rsqrt(ms + EPS)
                o_ref[c * rc:(c + 1) * rc, t * LANES:(t + 1) * LANES] = val.astype(BF16)

    @pl.when((j == COL_V) | (j == COL_PIN))
    def _():
        for c in range(n_chunks):
            o_ref[c * rc:(c + 1) * rc, :] = chunk_acc(c).astype(BF16)

    @pl.when(j >= COL_GA)
    def _():
        for c in range(n_chunks):
            o_ref[c * rc:(c + 1) * rc, :] = jax.nn.sigmoid(chunk_acc(c)).astype(BF16)


def _in_projection(x0, x1, posblk, gattn, w_in, cos_t, sin_t, qkg, bd, *, tm, tn, rc):
    t0, t1 = x0.shape[0], x1.shape[0]
    nt0, nt1 = t0 // tm, t1 // tm
    nt = nt0 + nt1
    grid_spec = pltpu.PrefetchScalarGridSpec(
        num_scalar_prefetch=1,
        grid=(nt, IN_COLS // tn),
        in_specs=[
            pl.BlockSpec((tm, D_MODEL), lambda i, j, pb: (jnp.minimum(i, nt0 - 1), 0)),
            pl.BlockSpec((tm, D_MODEL), lambda i, j, pb: (jnp.maximum(i - nt0, 0), 0)),
            pl.BlockSpec((1, D_MODEL), lambda i, j, pb: (0, 0)),
            pl.BlockSpec((D_MODEL, tn), lambda i, j, pb: (0, j)),
            pl.BlockSpec((tm, LANES), lambda i, j, pb: (pb[i], 0)),
            pl.BlockSpec((tm, LANES), lambda i, j, pb: (pb[i], 0)),
            pl.BlockSpec((8, tn), lambda i, j, pb: (0, 0)),
            pl.BlockSpec((LANES, LANES), lambda i, j, pb: (0, 0)),
        ],
        out_specs=pl.BlockSpec((tm, tn), lambda i, j, pb: (i, j)),
        scratch_shapes=[pltpu.VMEM((tm, D_MODEL), BF16)],
    )
    return pl.pallas_call(
        functools.partial(_inproj_kernel, nt0=nt0, rc=rc),
        grid_spec=grid_spec,
        out_shape=jax.ShapeDtypeStruct((t0 + t1, IN_COLS), BF16),
        compiler_params=_cparams(("arbitrary", "arbitrary")),
        name="in_projection",
    )(posblk, x0, x1, gattn, w_in, cos_t, sin_t, qkg, bd)


def _attn_kernel(q_ref, k_ref, v_ref, lam_ref, sg_ref, o_ref, vt_ref, acc_ref, m_ref, l_ref, kn_ref,
                 *, tk, lam_init):
    i = pl.program_id(2)
    s_len = k_ref.shape[0]
    tq = q_ref.shape[0]
    nk = s_len // tk
    nt_dims = (((1,), (1,)), ((), ()))

    sel_r = lax.broadcasted_iota(I32, (8, HEAD_W), 0)
    sel_l = lax.broadcasted_iota(I32, (8, HEAD_W), 1)
    sel = jnp.where(sel_l // HEAD_DIM == sel_r, 1.0, 0.0).astype(BF16)

    @pl.when(i == 0)
    def _():
        kn2 = jnp.zeros((8, 1), F32)
        for jb in range(nk):
            rows = slice(jb * tk, (jb + 1) * tk)
            vt_ref[:, rows] = v_ref[rows, :].astype(F32).T.astype(BF16)
            kf = k_ref[rows, :].astype(F32)
            n2 = lax.dot_general(sel, (kf * kf).astype(BF16), nt_dims, preferred_element_type=F32)
            kn2 = jnp.maximum(kn2, jnp.max(n2, axis=1, keepdims=True))
        kn_ref[...] = jnp.broadcast_to(kn2, kn_ref.shape)

    q = q_ref[...]
    lane = lax.broadcasted_iota(I32, q.shape, 1)
    zero = jnp.zeros_like(q)
    qc = (jnp.where(lane < HEAD_DIM, q, zero), jnp.where(lane >= HEAD_DIM, q, zero))

    qf = q.astype(F32)
    qn2 = lax.dot_general(sel, (qf * qf).astype(BF16), nt_dims, preferred_element_type=F32)
    bound = jnp.sqrt(qn2 * kn_ref[:, 0:1]) * SCORE_BOUND_MARGIN
    bound_c = (bound[0:1, :], bound[1:2, :])
    use_bound = jnp.max(bound[0:2, :]) <= SAFE_SCORE_BOUND

    l_ref[...] = jnp.zeros(l_ref.shape, F32)
    acc_ref[...] = jnp.zeros(acc_ref.shape, F32)

    def load_blocks(jb):
        off = pl.multiple_of(jb * tk, tk)
        return k_ref[pl.ds(off, tk), :], vt_ref[:, pl.ds(off, tk)]

    def bounded_step(jb, carry):
        kb, vtb = load_blocks(jb)
        for c in range(2):
            st = lax.dot_general(kb, qc[c], nt_dims, preferred_element_type=F32)
            p = jnp.exp2(st - bound_c[c])
            l_ref[c] += jnp.sum(p, axis=0, keepdims=True)
            acc_ref[c] += jnp.dot(vtb, p.astype(BF16), preferred_element_type=F32)
        return carry

    def running_max_step(jb, carry):
        kb, vtb = load_blocks(jb)
        for c in range(2):
            st = lax.dot_general(kb, qc[c], nt_dims, preferred_element_type=F32)
            m_old = m_ref[c]
            m_new = jnp.maximum(m_old, jnp.max(st, axis=0, keepdims=True))
            alpha = jnp.exp2(m_old - m_new)
            p = jnp.exp2(st - m_new)
            l_ref[c] = alpha * l_ref[c] + jnp.sum(p, axis=0, keepdims=True)
            acc_ref[c] = alpha * acc_ref[c] + jnp.dot(vtb, p.astype(BF16), preferred_element_type=F32)
            m_ref[c] = m_new
        return carry

    @pl.when(use_bound)
    def _():
        lax.fori_loop(0, nk, bounded_step, 0)

    @pl.when(jnp.logical_not(use_bound))
    def _():
        m_ref[...] = jnp.full(m_ref.shape, -jnp.inf, F32)
        lax.fori_loop(0, nk, running_max_step, 0)

    lq1, lk1, lq2, lk2 = lam_ref[0:1, :], lam_ref[1:2, :], lam_ref[2:3, :], lam_ref[3:4, :]
    lam = (jnp.exp(jnp.sum(lq1 * lk1, axis=-1, keepdims=True))
           - jnp.exp(jnp.sum(lq2 * lk2, axis=-1, keepdims=True)) + lam_init)
    ot = acc_ref[0] / l_ref[0] - lam * (acc_ref[1] / l_ref[1])
    ms = jnp.mean(ot * ot, axis=0, keepdims=True)
    on = (ot * lax.rsqrt(ms + EPS)).T
    o_ref[...] = (on * sg_ref[...] * (1.0 - lam_init)).astype(BF16)


def _attention(proj, tok_off, batch, s_len, lam_pack, subln_g, *, tq, tk, lam_init):
    qb0 = tok_off // tq
    sb0 = tok_off // s_len
    nq = s_len // tq
    return pl.pallas_call(
        functools.partial(_attn_kernel, tk=tk, lam_init=lam_init),
        grid=(batch, N_HEADS, nq),
        in_specs=[
            pl.BlockSpec((tq, HEAD_W), lambda b, h, i: (qb0 + b * nq + i, COL_Q * N_HEADS + h)),
            pl.BlockSpec((s_len, HEAD_W), lambda b, h, i: (sb0 + b, COL_K * N_HEADS + h)),
            pl.BlockSpec((s_len, HEAD_W), lambda b, h, i: (sb0 + b, COL_V * N_HEADS + h)),
            pl.BlockSpec((8, HEAD_DIM), lambda b, h, i: (0, 0)),
            pl.BlockSpec((1, HEAD_W), lambda b, h, i: (0, 0)),
        ],
        out_specs=pl.BlockSpec((tq, HEAD_W), lambda b, h, i: (b * nq + i, h)),
        out_shape=jax.ShapeDtypeStruct((batch * s_len, ATTN_W), BF16),
        scratch_shapes=[
            pltpu.VMEM((HEAD_W, s_len), BF16),
            pltpu.VMEM((2, HEAD_W, tq), F32),
            pltpu.VMEM((2, 1, tq), F32),
            pltpu.VMEM((2, 1, tq), F32),
            pltpu.VMEM((8, LANES), F32),
        ],
        compiler_params=_cparams(("arbitrary", "arbitrary", "arbitrary")),
        name="diff_attention",
    )(proj, proj, proj, lam_pack, subln_g)


def _post_kernel(pos_ref, len_ref, x0_ref, x1_ref, o0_ref, o1_ref, pin_ref, prev_ref, next_ref,
                 ga_ref, gp_ref, wau_ref, wgrp_ref, pscale_ref, wpu_ref, wout_ref, gffn_ref,
                 wrh_ref, wrl_ref, br_ref, xnew_ref, ids_ref, wts_ref, *, nt0):
    i = pl.program_id(0)
    tm = pin_ref.shape[0]
    p0 = pos_ref[i]
    s_len = len_ref[i]
    first = i < nt0

    o = jnp.where(first, o0_ref[...], o1_ref[...])
    attn_d = jnp.dot(o, wau_ref[...], preferred_element_type=F32)

    ext = jnp.concatenate([prev_ref[...], pin_ref[...], next_ref[...]], axis=0)
    ke = tm + 2 * POOL_HALO
    r_band = lax.broadcasted_iota(I32, (tm, ke), 0) + p0
    c_band = lax.broadcasted_iota(I32, (tm, ke), 1) + (p0 - POOL_HALO)
    r_cnt = lax.broadcasted_iota(I32, (tm, POOL_GROUP_W), 0) + p0
    pooled = []
    for gi, w in enumerate(POOL_WINDOWS):
        back, fwd = w // 2, w - 1 - w // 2
        lo = jnp.maximum(r_band - back, 0)
        hi = jnp.minimum(r_band + fwd, s_len - 1)
        band = jnp.where((c_band >= lo) & (c_band <= hi), 1.0, 0.0).astype(BF16)
        cnt = jnp.minimum(r_cnt + fwd, s_len - 1) - jnp.maximum(r_cnt - back, 0) + 1
        cols = slice(gi * POOL_GROUP_W, (gi + 1) * POOL_GROUP_W)
        wsum = jnp.dot(band, ext[:, cols], preferred_element_type=F32)
        y = wsum / cnt.astype(F32) - pin_ref[:, cols].astype(F32)
        pooled.append(jnp.dot(y.astype(BF16), wgrp_ref[gi], preferred_element_type=F32))
    pooled = (jnp.concatenate(pooled, axis=1) * pscale_ref[...]).astype(BF16)
    pool_d = jnp.dot(pooled, wpu_ref[...], preferred_element_type=F32)

    merged = ga_ref[...].astype(F32) * attn_d + gp_ref[...].astype(F32) * pool_d
    x = jnp.where(first, x0_ref[...], x1_ref[...])
    xnew = x + jnp.dot(merged.astype(BF16), wout_ref[...], preferred_element_type=F32)
    _store_token_rows(xnew_ref, xnew)

    ms = jnp.mean(xnew * xnew, axis=-1, keepdims=True)
    hn = xnew * lax.rsqrt(ms + EPS) * gffn_ref[...]
    hi_part = hn.astype(BF16)
    lo_part = (hn - hi_part.astype(F32)).astype(BF16)
    logits = (jnp.dot(hi_part, wrh_ref[...], preferred_element_type=F32)
              + (jnp.dot(hi_part, wrl_ref[...], preferred_element_type=F32)
                 + jnp.dot(lo_part, wrh_ref[...], preferred_element_type=F32))
              + br_ref[...])
    lt = logits.T

    gl = lt[0:N_GROUPS, :]
    g_iota = lax.broadcasted_iota(I32, gl.shape, 0)
    g_max = jnp.max(gl, axis=0, keepdims=True)
    g_sel = jnp.min(jnp.where(gl == g_max, g_iota, N_GROUPS), axis=0, keepdims=True)
    g_w = 1.0 / jnp.sum(jnp.exp(gl - g_max), axis=0, keepdims=True)
    el = lt[EXPERT_COL0:EXPERT_COL0 + N_EXPERTS, :]
    e_iota = lax.broadcasted_iota(I32, el.shape, 0)
    neg = jnp.float32(-jnp.inf)
    el = jnp.where(e_iota // EXPERTS_PER_GROUP == g_sel, el, neg)
    v1 = jnp.max(el, axis=0, keepdims=True)
    i1 = jnp.min(jnp.where(el == v1, e_iota, N_EXPERTS), axis=0, keepdims=True)
    el2 = jnp.where(e_iota == i1, neg, el)
    v2 = jnp.max(el2, axis=0, keepdims=True)
    i2 = jnp.min(jnp.where(el2 == v2, e_iota, N_EXPERTS), axis=0, keepdims=True)
    d = jnp.exp(v2 - v1)
    w1 = g_w / (1.0 + d)
    w2 = g_w * d / (1.0 + d)
    row = lax.broadcasted_iota(I32, (8, tm), 0)
    ids_ref[...] = jnp.where(row == 0, i1, jnp.where(row == 1, i2, 0))
    wts_ref[...] = jnp.where(row == 0, w1, jnp.where(row == 1, w2, 0.0))


def _post_block(x0, x1, o0, o1, proj, tile_pos, tile_len, wau, wgrp, pscale, wpu, wout, gffn,
                wrh, wrl, br, *, tm):
    t0, t1 = x0.shape[0], x1.shape[0]
    t = t0 + t1
    nt0 = t0 // tm
    nt = t // tm
    hb = tm // POOL_HALO
    n_hblk = t // POOL_HALO
    const = lambda *shape: pl.BlockSpec(shape, lambda i, tp, tl: (0,) * len(shape),
                                        pipeline_mode=pl.Buffered(1))
    grid_spec = pltpu.PrefetchScalarGridSpec(
        num_scalar_prefetch=2,
        grid=(nt,),
        in_specs=[
            pl.BlockSpec((tm, D_MODEL), lambda i, tp, tl: (jnp.minimum(i, nt0 - 1), 0)),
            pl.BlockSpec((tm, D_MODEL), lambda i, tp, tl: (jnp.maximum(i - nt0, 0), 0)),
            pl.BlockSpec((tm, ATTN_W), lambda i, tp, tl: (jnp.minimum(i, nt0 - 1), 0)),
            pl.BlockSpec((tm, ATTN_W), lambda i, tp, tl: (jnp.maximum(i - nt0, 0), 0)),
            pl.BlockSpec((tm, POOL_W), lambda i, tp, tl: (i, COL_PIN)),
            pl.BlockSpec((POOL_HALO, POOL_W), lambda i, tp, tl: (jnp.maximum(i * hb - 1, 0), COL_PIN)),
            pl.BlockSpec((POOL_HALO, POOL_W), lambda i, tp, tl: (jnp.minimum((i + 1) * hb, n_hblk - 1), COL_PIN)),
            pl.BlockSpec((tm, D_MODEL), lambda i, tp, tl: (i, COL_GA // 2)),
            pl.BlockSpec((tm, D_MODEL), lambda i, tp, tl: (i, COL_GP // 2)),
            const(ATTN_W, D_MODEL),
            const(len(POOL_WINDOWS), POOL_GROUP_W, POOL_GROUP_W),
            const(1, POOL_W),
            const(POOL_W, D_MODEL),
            const(D_MODEL, D_MODEL),
            const(1, D_MODEL),
            const(D_MODEL, ROUTER_COLS),
            const(D_MODEL, ROUTER_COLS),
            const(1, ROUTER_COLS),
        ],
        out_specs=[
            pl.BlockSpec((tm * ROW_TILES, LANES), lambda i, tp, tl: (i, 0)),
            pl.BlockSpec((8, tm), lambda i, tp, tl: (0, i)),
            pl.BlockSpec((8, tm), lambda i, tp, tl: (0, i)),
        ],
    )
    return pl.pallas_call(
        functools.partial(_post_kernel, nt0=nt0),
        grid_spec=grid_spec,
        out_shape=[
            jax.ShapeDtypeStruct((t * ROW_TILES, LANES), F32),
            jax.ShapeDtypeStruct((8, t), I32),
            jax.ShapeDtypeStruct((8, t), F32),
        ],
        compiler_params=_cparams(("arbitrary",)),
        name="post_block",
    )(tile_pos, tile_len, x0, x1, o0, o1, proj, proj, proj, proj, proj, wau, wgrp, pscale, wpu,
      wout, gffn, wrh, wrl, br)


GATHER_UNROLL = 8


def _row_gather_start(src_hbm, idx_ref, base, n_rows, buf, sem):
    def body(r, carry):
        src0 = pl.multiple_of(idx_ref[base + r] * ROW_TILES, ROW_TILES)
        dst0 = pl.multiple_of(r * ROW_TILES, ROW_TILES)
        pltpu.make_async_copy(src_hbm.at[pl.ds(src0, ROW_TILES)], buf.at[pl.ds(dst0, ROW_TILES)], sem).start()
        return carry
    lax.fori_loop(0, n_rows, body, 0, unroll=GATHER_UNROLL)


def _row_gather_wait(src_hbm, n_rows, buf, sem):
    pltpu.make_async_copy(src_hbm.at[pl.ds(0, n_rows * ROW_TILES)], buf, sem).wait()


def _moe_kernel(te_ref, tf_ref, nu_ref, src_ref, x_hbm, gffn_ref, ws_ref, wg_ref, wu_ref, wd_ref,
                y_ref, xbuf, sem, wgb, wub, wdb):
    del te_ref
    i = pl.program_id(0)
    tm = y_ref.shape[0] // ROW_TILES
    n_used = nu_ref[0]
    slot = lax.rem(i, 2)

    @pl.when(i == 0)
    def _():
        _row_gather_start(x_hbm, src_ref, 0, tm, xbuf.at[0], sem.at[0])

    @pl.when(i + 1 < n_used)
    def _():
        _row_gather_start(x_hbm, src_ref, (i + 1) * tm, tm, xbuf.at[1 - slot], sem.at[1 - slot])

    @pl.when(i < n_used)
    def _():
        @pl.when(tf_ref[i] == 1)
        def _():
            wgb[...] = wg_ref[0].astype(BF16)
            wub[...] = wu_ref[0].astype(BF16)
            wdb[...] = wd_ref[0].astype(BF16)

        _row_gather_wait(x_hbm, tm, xbuf.at[slot], sem.at[slot])
        xs = _load_token_rows(xbuf.at[slot], tm)
        ms = jnp.mean(xs * xs, axis=-1, keepdims=True)
        hn = (xs * lax.rsqrt(ms + EPS) * gffn_ref[...]).astype(BF16)
        gate = jnp.dot(hn, wgb[...], preferred_element_type=F32)
        up = jnp.dot(hn, wub[...], preferred_element_type=F32)
        h = jax.nn.silu(gate) * up * ws_ref[...]
        _store_token_rows(y_ref, jnp.dot(h.astype(BF16), wdb[...], preferred_element_type=F32))

    @pl.when(i >= n_used)
    def _():
        y_ref[...] = jnp.zeros(y_ref.shape, y_ref.dtype)


def _moe_ffn(xnew, tile_e, tile_first, n_used, src_row, w_slot, gffn, w_gate, w_up, w_down, *, tm):
    p_rows = src_row.shape[0]
    n_tiles = p_rows // tm
    grid_spec = pltpu.PrefetchScalarGridSpec(
        num_scalar_prefetch=4,
        grid=(n_tiles,),
        in_specs=[
            pl.BlockSpec(memory_space=pl.ANY),
            pl.BlockSpec((1, D_MODEL), lambda i, te, tf, nu, sr: (0, 0)),
            pl.BlockSpec((tm, 1), lambda i, te, tf, nu, sr: (i, 0)),
            pl.BlockSpec((1, D_MODEL, D_EXPERT), lambda i, te, tf, nu, sr: (te[i], 0, 0)),
            pl.BlockSpec((1, D_MODEL, D_EXPERT), lambda i, te, tf, nu, sr: (te[i], 0, 0)),
            pl.BlockSpec((1, D_EXPERT, D_MODEL), lambda i, te, tf, nu, sr: (te[i], 0, 0)),
        ],
        out_specs=pl.BlockSpec((tm * ROW_TILES, LANES), lambda i, te, tf, nu, sr: (i, 0)),
        scratch_shapes=[
            pltpu.VMEM((2, tm * ROW_TILES, LANES), F32),
            pltpu.SemaphoreType.DMA((2,)),
            pltpu.VMEM((D_MODEL, D_EXPERT), BF16),
            pltpu.VMEM((D_MODEL, D_EXPERT), BF16),
            pltpu.VMEM((D_EXPERT, D_MODEL), BF16),
        ],
    )
    return pl.pallas_call(
        _moe_kernel,
        grid_spec=grid_spec,
        out_shape=jax.ShapeDtypeStruct((p_rows * ROW_TILES, LANES), F32),
        compiler_params=_cparams(("arbitrary",)),
        name="moe_ffn",
    )(tile_e, tile_first, n_used, src_row, xnew, gffn, w_slot, w_gate, w_up, w_down)


def _combine_kernel(pos_ref, xnew_ref, y_hbm, out0_ref, out1_ref, ybuf, sem, *, nt0, n_tok):
    i = pl.program_id(0)
    nt = pl.num_programs(0)
    tm = xnew_ref.shape[0] // ROW_TILES
    slot = lax.rem(i, 2)

    def start(tile, s):
        _row_gather_start(y_hbm, pos_ref, tile * tm, tm, ybuf.at[s, 0], sem.at[s])
        _row_gather_start(y_hbm, pos_ref, n_tok + tile * tm, tm, ybuf.at[s, 1], sem.at[s])

    @pl.when(i == 0)
    def _():
        start(0, 0)

    @pl.when(i + 1 < nt)
    def _():
        start(i + 1, 1 - slot)

    _row_gather_wait(y_hbm, tm, ybuf.at[slot, 0], sem.at[slot])
    _row_gather_wait(y_hbm, tm, ybuf.at[slot, 1], sem.at[slot])
    def write(out_ref):
        for s in range(ROW_TILES):
            out_ref[:, s * LANES:(s + 1) * LANES] = (
                _token_slab(xnew_ref, tm, s)
                + (_token_slab(ybuf.at[slot, 0], tm, s) + _token_slab(ybuf.at[slot, 1], tm, s)))

    @pl.when(i < nt0)
    def _():
        write(out0_ref)

    @pl.when(i >= nt0)
    def _():
        write(out1_ref)


def _combine(xnew, y_sorted, pos, t0, *, tm):
    t = xnew.shape[0] // ROW_TILES
    nt0 = t0 // tm
    nt = t // tm
    grid_spec = pltpu.PrefetchScalarGridSpec(
        num_scalar_prefetch=1,
        grid=(nt,),
        in_specs=[
            pl.BlockSpec((tm * ROW_TILES, LANES), lambda i, ps: (i, 0)),
            pl.BlockSpec(memory_space=pl.ANY),
        ],
        out_specs=[
            pl.BlockSpec((tm, D_MODEL), lambda i, ps: (jnp.minimum(i, nt0 - 1), 0)),
            pl.BlockSpec((tm, D_MODEL), lambda i, ps: (jnp.maximum(i - nt0, 0), 0)),
        ],
        scratch_shapes=[
            pltpu.VMEM((2, 2, tm * ROW_TILES, LANES), F32),
            pltpu.SemaphoreType.DMA((2,)),
        ],
    )
    return pl.pallas_call(
        functools.partial(_combine_kernel, nt0=nt0, n_tok=t),
        grid_spec=grid_spec,
        out_shape=[
            jax.ShapeDtypeStruct((t0, D_MODEL), F32),
            jax.ShapeDtypeStruct((t - t0, D_MODEL), F32),
        ],
        compiler_params=_cparams(("arbitrary",)),
        name="moe_combine",
    )(pos, xnew, y_sorted)


def _rope_tables(max_len):
    inv = 1.0 / (ROPE_THETA ** (jnp.arange(0, HEAD_DIM, 2, dtype=F32) / HEAD_DIM))
    ang = jnp.arange(max_len, dtype=F32)[:, None] * inv[None, :]
    ang = jnp.concatenate([ang, ang, ang, ang], axis=-1)
    lane = jnp.arange(LANES)
    sign = jnp.where((lane % HEAD_DIM) < HEAD_DIM // 2, -1.0, 1.0).astype(F32)
    return jnp.cos(ang), jnp.sin(ang) * sign[None, :]


def _tile_tables(groups, tm):
    pos, length = [], []
    for batch, s_len in groups:
        for _ in range(batch):
            for p in range(0, s_len, tm):
                pos.append(p)
                length.append(s_len)
    return np.asarray(pos, np.int32), np.asarray(length, np.int32)


def _routing_tables(ids, wts, n_tok, tm):
    e_flat = ids[0:2, :].reshape(-1)
    w_flat = wts[0:2, :].reshape(-1)
    onehot = (e_flat[:, None] == jnp.arange(N_EXPERTS, dtype=I32)[None, :]).astype(I32)
    csum = jnp.cumsum(onehot, axis=0)
    rank = jnp.sum(csum * onehot, axis=1) - 1
    counts = csum[-1]
    padded = ((counts + tm - 1) // tm) * tm
    ends = jnp.cumsum(padded)
    starts = ends - padded
    dst = (starts[e_flat] + rank).astype(I32)
    p_rows = 2 * n_tok + N_EXPERTS * tm
    n_tiles = p_rows // tm
    tok = jnp.tile(jnp.arange(n_tok, dtype=I32), 2)
    src_row = jnp.zeros((p_rows,), I32).at[dst].set(tok)
    w_slot = jnp.zeros((p_rows,), F32).at[dst].set(w_flat)
    n_used = (ends[-1] // tm).astype(I32)
    tile_start = jnp.arange(n_tiles, dtype=I32) * tm
    tile_e = jnp.sum((ends[None, :] <= tile_start[:, None]).astype(I32), axis=1)
    tile_e = jnp.minimum(tile_e, N_EXPERTS - 1)
    active = jnp.arange(n_tiles, dtype=I32) < n_used
    last_e = tile_e[jnp.maximum(n_used - 1, 0)]
    tile_e = jnp.where(active, tile_e, last_e)
    tile_first = (active & (tile_start == starts[tile_e])).astype(I32)
    return tile_e, tile_first, n_used.reshape(1), src_row, w_slot[:, None], dst


def _forward(x0, x1, groups, params, *, tm_proj, tn_proj, rc_proj, tq, tk, tm_post, tm_moe, tm_comb):
    (attn_norm_g, w_in, q_norm_g, k_norm_g, lambda_q1, lambda_k1, lambda_q2, lambda_k2, subln_g,
     w_attn_up, w_pool_grp, pool_scale, w_pool_up, w_out, ffn_norm_g, w_group_router,
     b_group_router, w_expert_router, b_expert_router, w_gate, w_up, w_down) = params
    t0, t1 = x0.shape[0], x1.shape[0]
    n_tok = t0 + t1
    lam_init = 0.8 - 0.6 * math.exp(-0.3 * 0)
    max_len = max(s for _, s in groups)

    cos_t, sin_t = _rope_tables(max_len)
    pos_proj, _ = _tile_tables(groups, tm_proj)
    q_scale = HEAD_DIM ** -0.5 * math.log2(math.e)
    qkg = jnp.zeros((8, ATTN_W), F32)
    qkg = qkg.at[0].set(jnp.tile(q_norm_g, ATTN_W // HEAD_DIM) * q_scale)
    qkg = qkg.at[1].set(jnp.tile(k_norm_g, ATTN_W // HEAD_DIM))
    seg = np.arange(LANES) // HEAD_DIM
    bd = jnp.asarray((seg[:, None] == seg[None, :]).astype(np.float32) / HEAD_DIM, BF16)

    proj = _in_projection(x0, x1, jnp.asarray(pos_proj // tm_proj), attn_norm_g[None, :],
                          w_in.astype(BF16), cos_t, sin_t, qkg, bd, tm=tm_proj, tn=tn_proj, rc=rc_proj)

    lam_pack = jnp.zeros((8, HEAD_DIM), F32)
    lam_pack = lam_pack.at[0].set(lambda_q1).at[1].set(lambda_k1).at[2].set(lambda_q2).at[3].set(lambda_k2)
    outs = []
    tok_off = 0
    for batch, s_len in groups:
        outs.append(_attention(proj, tok_off, batch, s_len, lam_pack, subln_g[None, :],
                               tq=min(tq, s_len), tk=min(tk, s_len), lam_init=lam_init))
        tok_off += batch * s_len

    tile_pos, tile_len = _tile_tables(groups, tm_post)
    w_router = jnp.zeros((D_MODEL, ROUTER_COLS), F32)
    w_router = w_router.at[:, 0:N_GROUPS].set(w_group_router)
    w_router = w_router.at[:, EXPERT_COL0:EXPERT_COL0 + N_EXPERTS].set(w_expert_router)
    b_router = jnp.zeros((1, ROUTER_COLS), F32)
    b_router = b_router.at[0, 0:N_GROUPS].set(b_group_router)
    b_router = b_router.at[0, EXPERT_COL0:EXPERT_COL0 + N_EXPERTS].set(b_expert_router)
    wr_hi = w_router.astype(BF16)
    wr_lo = (w_router - wr_hi.astype(F32)).astype(BF16)
    xnew, ids, wts = _post_block(
        x0, x1, outs[0], outs[1], proj, jnp.asarray(tile_pos), jnp.asarray(tile_len),
        w_attn_up.astype(BF16), w_pool_grp.astype(BF16), pool_scale[None, :], w_pool_up.astype(BF16),
        w_out.astype(BF16), ffn_norm_g[None, :], wr_hi, wr_lo, b_router, tm=tm_post)

    tile_e, tile_first, n_used, src_row, w_slot, dst = _routing_tables(ids, wts, n_tok, tm_moe)
    y_sorted = _moe_ffn(
        xnew, tile_e, tile_first, n_used, src_row, w_slot, ffn_norm_g[None, :],
        w_gate.reshape(N_EXPERTS, D_MODEL, D_EXPERT), w_up.reshape(N_EXPERTS, D_MODEL, D_EXPERT),
        w_down.reshape(N_EXPERTS, D_EXPERT, D_MODEL), tm=tm_moe)
    return _combine(xnew, y_sorted, dst, t0, tm=tm_comb)


def kernel(x_prompt, x_sample, attn_norm_g, w_in, q_norm_g, k_norm_g, lambda_q1, lambda_k1, lambda_q2,
           lambda_k2, subln_g, w_attn_up, w_pool_grp, pool_scale, w_pool_up, w_out, ffn_norm_g,
           w_group_router, b_group_router, w_expert_router, b_expert_router, w_gate, w_up, w_down):
    assert attn_norm_g.shape[0] == 1, "single-layer stack"
    params = tuple(p[0] for p in (
        attn_norm_g, w_in, q_norm_g, k_norm_g, lambda_q1, lambda_k1, lambda_q2, lambda_k2, subln_g,
        w_attn_up, w_pool_grp, pool_scale, w_pool_up, w_out, ffn_norm_g, w_group_router,
        b_group_router, w_expert_router, b_expert_router, w_gate, w_up, w_down))
    groups = (x_prompt.shape[:2], x_sample.shape[:2])
    y0, y1 = _forward(
        x_prompt.reshape(-1, D_MODEL), x_sample.reshape(-1, D_MODEL), groups, params,
        tm_proj=512, tn_proj=1024, rc_proj=512, tq=512, tk=512, tm_post=256, tm_moe=256, tm_comb=256)
    return y0.reshape(x_prompt.shape), y1.reshape(x_sample.shape)
```

```python
import functools
import math

import numpy as np
import jax
import jax.numpy as jnp
from jax import lax
from jax.experimental import pallas as pl
from jax.experimental.pallas import tpu as pltpu

F32 = jnp.float32
BF16 = jnp.bfloat16
I32 = jnp.int32

D_MODEL = 2048
HEAD_DIM = 64
N_HEADS = 8
HEAD_W = 2 * HEAD_DIM
ATTN_W = N_HEADS * HEAD_W
POOL_W = 1024
POOL_WINDOWS = (2, 4, 8, 16)
POOL_GROUP_W = POOL_W // len(POOL_WINDOWS)
POOL_HALO = 64
IN_COLS = 3 * ATTN_W + POOL_W + 2 * D_MODEL
N_GROUPS = 4
EXPERTS_PER_GROUP = 8
N_EXPERTS = N_GROUPS * EXPERTS_PER_GROUP
D_EXPERT = 512
ROPE_THETA = 10000.0
EPS = 1e-6
LANES = 128
ROUTER_COLS = 128
EXPERT_COL0 = 8
ROW_TILES = D_MODEL // LANES
SAFE_SCORE_BOUND = 50.0
SCORE_BOUND_MARGIN = 1.02

COL_Q, COL_K, COL_V, COL_PIN, COL_GA, COL_GP = 0, 1, 2, 3, 4, 6

VMEM_LIMIT = 56 * 1024 * 1024


def _cparams(sem, vmem=VMEM_LIMIT):
    return pltpu.CompilerParams(dimension_semantics=sem, vmem_limit_bytes=vmem)


def _token_slab(ref, n_tok, s):
    return ref[pl.ds(s, n_tok, stride=ROW_TILES), :]


def _load_token_rows(ref, n_tok):
    return jnp.concatenate([_token_slab(ref, n_tok, s) for s in range(ROW_TILES)], axis=1)


def _store_token_rows(ref, value):
    n_tok = value.shape[0]
    for s in range(ROW_TILES):
        ref[pl.ds(s, n_tok, stride=ROW_TILES), :] = value[:, s * LANES:(s + 1) * LANES]


def _inproj_kernel(posblk_ref, x0_ref, x1_ref, gattn_ref, w_ref, cos_ref, sin_ref, qkg_ref, bd_ref,
                   o_ref, xn_ref, *, nt0, rc):
    del posblk_ref
    i = pl.program_id(0)
    j = pl.program_id(1)
    tm = xn_ref.shape[0]
    n_chunks = tm // rc

    def normalise(x_ref):
        for c in range(n_chunks):
            xs = x_ref[c * rc:(c + 1) * rc, :]
            ms = jnp.mean(xs * xs, axis=-1, keepdims=True)
            xn_ref[c * rc:(c + 1) * rc, :] = (xs * lax.rsqrt(ms + EPS) * gattn_ref[...]).astype(BF16)

    @pl.when((j == 0) & (i < nt0))
    def _():
        normalise(x0_ref)

    @pl.when((j == 0) & (i >= nt0))
    def _():
        normalise(x1_ref)

    def chunk_acc(c):
        return jnp.dot(xn_ref[c * rc:(c + 1) * rc, :], w_ref[...], preferred_element_type=F32)

    @pl.when(j <= COL_K)
    def _():
        g = jnp.where(j == COL_Q, qkg_ref[0:1, :], qkg_ref[1:2, :])
        lane = lax.broadcasted_iota(I32, (rc, LANES), 1)
        first_half = (lane % HEAD_DIM) < (HEAD_DIM // 2)
        for c in range(n_chunks):
            a = chunk_acc(c)
            cos = cos_ref[c * rc:(c + 1) * rc, :]
            sin = sin_ref[c * rc:(c + 1) * rc, :]
            for t in range(a.shape[1] // LANES):
                at = a[:, t * LANES:(t + 1) * LANES]
                ms = jnp.dot((at * at).astype(BF16), bd_ref[...], preferred_element_type=F32)
                u = at * g[:, t * LANES:(t + 1) * LANES]
                rot = jnp.where(first_half, pltpu.roll(u, LANES - HEAD_DIM // 2, 1),
                                pltpu.roll(u, HEAD_DIM // 2, 1))
                val = (u * cos + rot * sin) * lax.rsqrt(ms + EPS)
                o_ref[c * rc:(c + 1) * rc, t * LANES:(t + 1) * LANES] = val.astype(BF16)

    @pl.when((j == COL_V) | (j == COL_PIN))
    def _():
        for c in range(n_chunks):
            o_ref[c * rc:(c + 1) * rc, :] = chunk_acc(c).astype(BF16)

    @pl.when(j >= COL_GA)
    def _():
        for c in range(n_chunks):
            o_ref[c * rc:(c + 1) * rc, :] = jax.nn.sigmoid(chunk_acc(c)).astype(BF16)


def _in_projection(x0, x1, posblk, gattn, w_in, cos_t, sin_t, qkg, bd, *, tm, tn, rc):
    t0, t1 = x0.shape[0], x1.shape[0]
    nt0, nt1 = t0 // tm, t1 // tm
    nt = nt0 + nt1
    grid_spec = pltpu.PrefetchScalarGridSpec(
        num_scalar_prefetch=1,
        grid=(nt, IN_COLS // tn),
        in_specs=[
            pl.BlockSpec((tm, D_MODEL), lambda i, j, pb: (jnp.minimum(i, nt0 - 1), 0),
                         pipeline_mode=pl.Buffered(1)),
            pl.BlockSpec((tm, D_MODEL), lambda i, j, pb: (jnp.maximum(i - nt0, 0), 0),
                         pipeline_mode=pl.Buffered(1)),
            pl.BlockSpec((1, D_MODEL), lambda i, j, pb: (0, 0)),
            pl.BlockSpec((D_MODEL, tn), lambda i, j, pb: (0, j)),
            pl.BlockSpec((tm, LANES), lambda i, j, pb: (pb[i], 0)),
            pl.BlockSpec((tm, LANES), lambda i, j, pb: (pb[i], 0)),
            pl.BlockSpec((8, tn), lambda i, j, pb: (0, 0)),
            pl.BlockSpec((LANES, LANES), lambda i, j, pb: (0, 0)),
        ],
        out_specs=pl.BlockSpec((tm, tn), lambda i, j, pb: (i, j)),
        scratch_shapes=[pltpu.VMEM((tm, D_MODEL), BF16)],
    )
    return pl.pallas_call(
        functools.partial(_inproj_kernel, nt0=nt0, rc=rc),
        grid_spec=grid_spec,
        out_shape=jax.ShapeDtypeStruct((t0 + t1, IN_COLS), BF16),
        compiler_params=_cparams(("arbitrary", "arbitrary")),
        name="in_projection",
    )(posblk, x0, x1, gattn, w_in, cos_t, sin_t, qkg, bd)


def _attn_kernel(q_ref, k_ref, v_ref, lam_ref, sg_ref, o_ref, vt_ref, acc_ref, m_ref, l_ref, kn_ref,
                 p_ref, *, tk, lam_init):
    i = pl.program_id(2)
    s_len = k_ref.shape[0]
    tq = q_ref.shape[0]
    nk = s_len // tk
    nt_dims = (((1,), (1,)), ((), ()))

    sel_r = lax.broadcasted_iota(I32, (8, HEAD_W), 0)
    sel_l = lax.broadcasted_iota(I32, (8, HEAD_W), 1)
    sel = jnp.where(sel_l // HEAD_DIM == sel_r, 1.0, 0.0).astype(BF16)

    @pl.when(i == 0)
    def _():
        kn2 = jnp.zeros((8, 1), F32)
        for jb in range(nk):
            rows = slice(jb * tk, (jb + 1) * tk)
            vt_ref[:, rows] = v_ref[rows, :].astype(F32).T.astype(BF16)
            kf = k_ref[rows, :].astype(F32)
            n2 = lax.dot_general(sel, (kf * kf).astype(BF16), nt_dims, preferred_element_type=F32)
            kn2 = jnp.maximum(kn2, jnp.max(n2, axis=1, keepdims=True))
        kn_ref[...] = jnp.broadcast_to(kn2, kn_ref.shape)

    q = q_ref[...]
    lane = lax.broadcasted_iota(I32, q.shape, 1)
    zero = jnp.zeros_like(q)
    qc = (jnp.where(lane < HEAD_DIM, q, zero), jnp.where(lane >= HEAD_DIM, q, zero))

    qf = q.astype(F32)
    qn2 = lax.dot_general(sel, (qf * qf).astype(BF16), nt_dims, preferred_element_type=F32)
    bound = jnp.sqrt(qn2 * kn_ref[:, 0:1]) * SCORE_BOUND_MARGIN
    bound_c = (bound[0:1, :], bound[1:2, :])
    use_bound = jnp.max(bound[0:2, :]) <= SAFE_SCORE_BOUND

    l_ref[...] = jnp.zeros(l_ref.shape, F32)
    acc_ref[...] = jnp.zeros(acc_ref.shape, F32)

    def load_blocks(jb):
        off = jb * tk if isinstance(jb, int) else pl.multiple_of(jb * tk, tk)
        return k_ref[pl.ds(off, tk), :], vt_ref[:, pl.ds(off, tk)]

    def bounded_weights(jb, slot):
        kb, _ = load_blocks(jb)
        for c in range(2):
            st = lax.dot_general(kb, qc[c], nt_dims, preferred_element_type=F32)
            p = jnp.exp2(st - bound_c[c])
            l_ref[c] += jnp.sum(p, axis=0, keepdims=True)
            p_ref[slot, c] = p.astype(BF16)

    def weighted_values(jb, slot):
        _, vtb = load_blocks(jb)
        for c in range(2):
            acc_ref[c] += jnp.dot(vtb, p_ref[slot, c], preferred_element_type=F32)

    def bounded_pair(m, carry):
        j0 = 2 * m
        bounded_weights(j0 + 1, 1)
        weighted_values(j0, 0)
        bounded_weights(j0 + 2, 0)
        weighted_values(j0 + 1, 1)
        return carry

    def running_max_step(jb, carry):
        kb, vtb = load_blocks(jb)
        for c in range(2):
            st = lax.dot_general(kb, qc[c], nt_dims, preferred_element_type=F32)
            m_old = m_ref[c]
            m_new = jnp.maximum(m_old, jnp.max(st, axis=0, keepdims=True))
            alpha = jnp.exp2(m_old - m_new)
            p = jnp.exp2(st - m_new)
            l_ref[c] = alpha * l_ref[c] + jnp.sum(p, axis=0, keepdims=True)
            acc_ref[c] = alpha * acc_ref[c] + jnp.dot(vtb, p.astype(BF16), preferred_element_type=F32)
            m_ref[c] = m_new
        return carry

    @pl.when(use_bound)
    def _():
        bounded_weights(0, 0)
        lax.fori_loop(0, nk // 2 - 1, bounded_pair, 0)
        bounded_weights(nk - 1, 1)
        weighted_values(nk - 2, 0)
        weighted_values(nk - 1, 1)

    @pl.when(jnp.logical_not(use_bound))
    def _():
        m_ref[...] = jnp.full(m_ref.shape, -jnp.inf, F32)
        lax.fori_loop(0, nk, running_max_step, 0)

    lq1, lk1, lq2, lk2 = lam_ref[0:1, :], lam_ref[1:2, :], lam_ref[2:3, :], lam_ref[3:4, :]
    lam = (jnp.exp(jnp.sum(lq1 * lk1, axis=-1, keepdims=True))
           - jnp.exp(jnp.sum(lq2 * lk2, axis=-1, keepdims=True)) + lam_init)
    ot = acc_ref[0] / l_ref[0] - lam * (acc_ref[1] / l_ref[1])
    ms = jnp.mean(ot * ot, axis=0, keepdims=True)
    on = (ot * lax.rsqrt(ms + EPS)).T
    o_ref[...] = (on * sg_ref[...] * (1.0 - lam_init)).astype(BF16)


def _attention(proj, tok_off, batch, s_len, lam_pack, subln_g, *, tq, tk, lam_init):
    qb0 = tok_off // tq
    sb0 = tok_off // s_len
    nq = s_len // tq
    return pl.pallas_call(
        functools.partial(_attn_kernel, tk=tk, lam_init=lam_init),
        grid=(batch, N_HEADS, nq),
        in_specs=[
            pl.BlockSpec((tq, HEAD_W), lambda b, h, i: (qb0 + b * nq + i, COL_Q * N_HEADS + h)),
            pl.BlockSpec((s_len, HEAD_W), lambda b, h, i: (sb0 + b, COL_K * N_HEADS + h)),
            pl.BlockSpec((s_len, HEAD_W), lambda b, h, i: (sb0 + b, COL_V * N_HEADS + h)),
            pl.BlockSpec((8, HEAD_DIM), lambda b, h, i: (0, 0)),
            pl.BlockSpec((1, HEAD_W), lambda b, h, i: (0, 0)),
        ],
        out_specs=pl.BlockSpec((tq, HEAD_W), lambda b, h, i: (b * nq + i, h)),
        out_shape=jax.ShapeDtypeStruct((batch * s_len, ATTN_W), BF16),
        scratch_shapes=[
            pltpu.VMEM((HEAD_W, s_len), BF16),
            pltpu.VMEM((2, HEAD_W, tq), F32),
            pltpu.VMEM((2, 1, tq), F32),
            pltpu.VMEM((2, 1, tq), F32),
            pltpu.VMEM((8, LANES), F32),
            pltpu.VMEM((2, 2, tk, tq), BF16),
        ],
        compiler_params=_cparams(("arbitrary", "arbitrary", "arbitrary")),
        name="diff_attention",
    )(proj, proj, proj, lam_pack, subln_g)


def _post_kernel(pos_ref, len_ref, x0_ref, x1_ref, o0_ref, o1_ref, pin_ref, prev_ref, next_ref,
                 ga_ref, gp_ref, wau_ref, wgrp_ref, pscale_ref, wpu_ref, wout_ref, gffn_ref,
                 wrc_ref, br_ref, xnew_ref, ids_ref, wts_ref, *, nt0):
    i = pl.program_id(0)
    tm = pin_ref.shape[0]
    p0 = pos_ref[i]
    s_len = len_ref[i]
    first = i < nt0

    o = jnp.where(first, o0_ref[...], o1_ref[...])
    attn_d = jnp.dot(o, wau_ref[...], preferred_element_type=F32)

    ext = jnp.concatenate([prev_ref[...], pin_ref[...], next_ref[...]], axis=0)
    ke = tm + 2 * POOL_HALO
    r_band = lax.broadcasted_iota(I32, (tm, ke), 0) + p0
    c_band = lax.broadcasted_iota(I32, (tm, ke), 1) + (p0 - POOL_HALO)
    r_cnt = lax.broadcasted_iota(I32, (tm, POOL_GROUP_W), 0) + p0
    pooled = []
    for gi, w in enumerate(POOL_WINDOWS):
        back, fwd = w // 2, w - 1 - w // 2
        lo = jnp.maximum(r_band - back, 0)
        hi = jnp.minimum(r_band + fwd, s_len - 1)
        band = jnp.where((c_band >= lo) & (c_band <= hi), 1.0, 0.0).astype(BF16)
        cnt = jnp.minimum(r_cnt + fwd, s_len - 1) - jnp.maximum(r_cnt - back, 0) + 1
        cols = slice(gi * POOL_GROUP_W, (gi + 1) * POOL_GROUP_W)
        wsum = jnp.dot(band, ext[:, cols], preferred_element_type=F32)
        y = wsum / cnt.astype(F32) - pin_ref[:, cols].astype(F32)
        pooled.append(jnp.dot(y.astype(BF16), wgrp_ref[gi], preferred_element_type=F32))
    pooled = (jnp.concatenate(pooled, axis=1) * pscale_ref[...]).astype(BF16)
    pool_d = jnp.dot(pooled, wpu_ref[...], preferred_element_type=F32)

    merged = ga_ref[...].astype(F32) * attn_d + gp_ref[...].astype(F32) * pool_d
    x = jnp.where(first, x0_ref[...], x1_ref[...])
    xnew = x + jnp.dot(merged.astype(BF16), wout_ref[...], preferred_element_type=F32)
    _store_token_rows(xnew_ref, xnew)

    ms = jnp.mean(xnew * xnew, axis=-1, keepdims=True)
    hn = xnew * lax.rsqrt(ms + EPS) * gffn_ref[...]
    hi_part = hn.astype(BF16)
    lo_part = (hn - hi_part.astype(F32)).astype(BF16)
    hw = jnp.dot(hi_part, wrc_ref[...], preferred_element_type=F32)
    logits = (hw[:, 0:ROUTER_COLS]
              + (hw[:, ROUTER_COLS:]
                 + jnp.dot(lo_part, wrc_ref[:, 0:ROUTER_COLS], preferred_element_type=F32))
              + br_ref[...])
    lt = logits.T

    gl = lt[0:N_GROUPS, :]
    g_iota = lax.broadcasted_iota(I32, gl.shape, 0)
    g_max = jnp.max(gl, axis=0, keepdims=True)
    g_sel = jnp.min(jnp.where(gl == g_max, g_iota, N_GROUPS), axis=0, keepdims=True)
    g_w = 1.0 / jnp.sum(jnp.exp(gl - g_max), axis=0, keepdims=True)
    el = lt[EXPERT_COL0:EXPERT_COL0 + N_EXPERTS, :]
    e_iota = lax.broadcasted_iota(I32, el.shape, 0)
    neg = jnp.float32(-jnp.inf)
    el = jnp.where(e_iota // EXPERTS_PER_GROUP == g_sel, el, neg)
    v1 = jnp.max(el, axis=0, keepdims=True)
    i1 = jnp.min(jnp.where(el == v1, e_iota, N_EXPERTS), axis=0, keepdims=True)
    el2 = jnp.where(e_iota == i1, neg, el)
    v2 = jnp.max(el2, axis=0, keepdims=True)
    i2 = jnp.min(jnp.where(el2 == v2, e_iota, N_EXPERTS), axis=0, keepdims=True)
    d = jnp.exp(v2 - v1)
    w1 = g_w / (1.0 + d)
    w2 = g_w * d / (1.0 + d)
    row = lax.broadcasted_iota(I32, (8, tm), 0)
    ids_ref[...] = jnp.where(row == 0, i1, jnp.where(row == 1, i2, 0))
    wts_ref[...] = jnp.where(row == 0, w1, jnp.where(row == 1, w2, 0.0))


def _post_block(x0, x1, o0, o1, proj, tile_pos, tile_len, wau, wgrp, pscale, wpu, wout, gffn,
                wrc, br, *, tm):
    t0, t1 = x0.shape[0], x1.shape[0]
    t = t0 + t1
    nt0 = t0 // tm
    nt = t // tm
    hb = tm // POOL_HALO
    n_hblk = t // POOL_HALO
    const = lambda *shape: pl.BlockSpec(shape, lambda i, tp, tl: (0,) * len(shape),
                                        pipeline_mode=pl.Buffered(1))
    grid_spec = pltpu.PrefetchScalarGridSpec(
        num_scalar_prefetch=2,
        grid=(nt,),
        in_specs=[
            pl.BlockSpec((tm, D_MODEL), lambda i, tp, tl: (jnp.minimum(i, nt0 - 1), 0)),
            pl.BlockSpec((tm, D_MODEL), lambda i, tp, tl: (jnp.maximum(i - nt0, 0), 0)),
            pl.BlockSpec((tm, ATTN_W), lambda i, tp, tl: (jnp.minimum(i, nt0 - 1), 0)),
            pl.BlockSpec((tm, ATTN_W), lambda i, tp, tl: (jnp.maximum(i - nt0, 0), 0)),
            pl.BlockSpec((tm, POOL_W), lambda i, tp, tl: (i, COL_PIN)),
            pl.BlockSpec((POOL_HALO, POOL_W), lambda i, tp, tl: (jnp.maximum(i * hb - 1, 0), COL_PIN)),
            pl.BlockSpec((POOL_HALO, POOL_W), lambda i, tp, tl: (jnp.minimum((i + 1) * hb, n_hblk - 1), COL_PIN)),
            pl.BlockSpec((tm, D_MODEL), lambda i, tp, tl: (i, COL_GA // 2)),
            pl.BlockSpec((tm, D_MODEL), lambda i, tp, tl: (i, COL_GP // 2)),
            const(ATTN_W, D_MODEL),
            const(len(POOL_WINDOWS), POOL_GROUP_W, POOL_GROUP_W),
            const(1, POOL_W),
            const(POOL_W, D_MODEL),
            const(D_MODEL, D_MODEL),
            const(1, D_MODEL),
            const(D_MODEL, 2 * ROUTER_COLS),
            const(1, ROUTER_COLS),
        ],
        out_specs=[
            pl.BlockSpec((tm * ROW_TILES, LANES), lambda i, tp, tl: (i, 0)),
            pl.BlockSpec((8, tm), lambda i, tp, tl: (0, i)),
            pl.BlockSpec((8, tm), lambda i, tp, tl: (0, i)),
        ],
    )
    return pl.pallas_call(
        functools.partial(_post_kernel, nt0=nt0),
        grid_spec=grid_spec,
        out_shape=[
            jax.ShapeDtypeStruct((t * ROW_TILES, LANES), F32),
            jax.ShapeDtypeStruct((8, t), I32),
            jax.ShapeDtypeStruct((8, t), F32),
        ],
        compiler_params=_cparams(("arbitrary",)),
        name="post_block",
    )(tile_pos, tile_len, x0, x1, o0, o1, proj, proj, proj, proj, proj, wau, wgrp, pscale, wpu,
      wout, gffn, wrc, br)


GATHER_UNROLL = 8


def _row_gather_start(src_hbm, idx_ref, base, n_rows, buf, sem):
    def body(r, carry):
        src0 = pl.multiple_of(idx_ref[base + r] * ROW_TILES, ROW_TILES)
        dst0 = pl.multiple_of(r * ROW_TILES, ROW_TILES)
        pltpu.make_async_copy(src_hbm.at[pl.ds(src0, ROW_TILES)], buf.at[pl.ds(dst0, ROW_TILES)], sem).start()
        return carry
    lax.fori_loop(0, n_rows, body, 0, unroll=GATHER_UNROLL)


def _row_gather_wait(src_hbm, n_rows, buf, sem):
    pltpu.make_async_copy(src_hbm.at[pl.ds(0, n_rows * ROW_TILES)], buf, sem).wait()


def _moe_kernel(te_ref, tf_ref, nu_ref, src_ref, x_hbm, gffn_ref, wg_ref, wu_ref, wd_ref,
                y_ref, xbuf, sem, wgb, wub, wdb):
    del te_ref
    i = pl.program_id(0)
    tm = y_ref.shape[0] // ROW_TILES
    n_used = nu_ref[0]
    slot = lax.rem(i, 2)

    @pl.when(i == 0)
    def _():
        _row_gather_start(x_hbm, src_ref, 0, tm, xbuf.at[0], sem.at[0])

    @pl.when(i + 1 < n_used)
    def _():
        _row_gather_start(x_hbm, src_ref, (i + 1) * tm, tm, xbuf.at[1 - slot], sem.at[1 - slot])

    @pl.when(i < n_used)
    def _():
        @pl.when(tf_ref[i] == 1)
        def _():
            wgb[...] = wg_ref[0].astype(BF16)
            wub[...] = wu_ref[0].astype(BF16)
            wdb[...] = wd_ref[0].astype(BF16)

        _row_gather_wait(x_hbm, tm, xbuf.at[slot], sem.at[slot])
        xs = _load_token_rows(xbuf.at[slot], tm)
        ms = jnp.mean(xs * xs, axis=-1, keepdims=True)
        hn = (xs * lax.rsqrt(ms + EPS) * gffn_ref[...]).astype(BF16)
        gate = jnp.dot(hn, wgb[...], preferred_element_type=F32)
        up = jnp.dot(hn, wub[...], preferred_element_type=F32)
        h = jax.nn.silu(gate) * up
        _store_token_rows(y_ref, jnp.dot(h.astype(BF16), wdb[...], preferred_element_type=F32))

    @pl.when(i >= n_used)
    def _():
        y_ref[...] = jnp.zeros(y_ref.shape, y_ref.dtype)


def _moe_ffn(xnew, tile_e, tile_first, n_used, src_row, gffn, w_gate, w_up, w_down, *, tm):
    p_rows = src_row.shape[0]
    n_tiles = p_rows // tm
    grid_spec = pltpu.PrefetchScalarGridSpec(
        num_scalar_prefetch=4,
        grid=(n_tiles,),
        in_specs=[
            pl.BlockSpec(memory_space=pl.ANY),
            pl.BlockSpec((1, D_MODEL), lambda i, te, tf, nu, sr: (0, 0)),
            pl.BlockSpec((1, D_MODEL, D_EXPERT), lambda i, te, tf, nu, sr: (te[i], 0, 0)),
            pl.BlockSpec((1, D_MODEL, D_EXPERT), lambda i, te, tf, nu, sr: (te[i], 0, 0)),
            pl.BlockSpec((1, D_EXPERT, D_MODEL), lambda i, te, tf, nu, sr: (te[i], 0, 0)),
        ],
        out_specs=pl.BlockSpec((tm * ROW_TILES, LANES), lambda i, te, tf, nu, sr: (i, 0)),
        scratch_shapes=[
            pltpu.VMEM((2, tm * ROW_TILES, LANES), F32),
            pltpu.SemaphoreType.DMA((2,)),
            pltpu.VMEM((D_MODEL, D_EXPERT), BF16),
            pltpu.VMEM((D_MODEL, D_EXPERT), BF16),
            pltpu.VMEM((D_EXPERT, D_MODEL), BF16),
        ],
    )
    return pl.pallas_call(
        _moe_kernel,
        grid_spec=grid_spec,
        out_shape=jax.ShapeDtypeStruct((p_rows * ROW_TILES, LANES), F32),
        compiler_params=_cparams(("arbitrary",)),
        name="moe_ffn",
    )(tile_e, tile_first, n_used, src_row, xnew, gffn, w_gate, w_up, w_down)


def _combine_kernel(pos_ref, xnew_ref, wts_ref, y_hbm, out0_ref, out1_ref, ybuf, sem, *, nt0, n_tok):
    i = pl.program_id(0)
    nt = pl.num_programs(0)
    tm = xnew_ref.shape[0] // ROW_TILES
    slot = lax.rem(i, 2)

    def start(tile, s):
        _row_gather_start(y_hbm, pos_ref, tile * tm, tm, ybuf.at[s, 0], sem.at[s])
        _row_gather_start(y_hbm, pos_ref, n_tok + tile * tm, tm, ybuf.at[s, 1], sem.at[s])

    @pl.when(i == 0)
    def _():
        start(0, 0)

    @pl.when(i + 1 < nt)
    def _():
        start(i + 1, 1 - slot)

    _row_gather_wait(y_hbm, tm, ybuf.at[slot, 0], sem.at[slot])
    _row_gather_wait(y_hbm, tm, ybuf.at[slot, 1], sem.at[slot])
    wt = wts_ref[...].T
    w1 = jnp.broadcast_to(wt[:, 0:1], (tm, LANES))
    w2 = jnp.broadcast_to(wt[:, 1:2], (tm, LANES))

    def write(out_ref):
        for s in range(ROW_TILES):
            out_ref[:, s * LANES:(s + 1) * LANES] = (
                _token_slab(xnew_ref, tm, s)
                + (w1 * _token_slab(ybuf.at[slot, 0], tm, s) + w2 * _token_slab(ybuf.at[slot, 1], tm, s)))

    @pl.when(i < nt0)
    def _():
        write(out0_ref)

    @pl.when(i >= nt0)
    def _():
        write(out1_ref)


def _combine(xnew, y_sorted, wts, pos, t0, *, tm):
    t = xnew.shape[0] // ROW_TILES
    nt0 = t0 // tm
    nt = t // tm
    grid_spec = pltpu.PrefetchScalarGridSpec(
        num_scalar_prefetch=1,
        grid=(nt,),
        in_specs=[
            pl.BlockSpec((tm * ROW_TILES, LANES), lambda i, ps: (i, 0)),
            pl.BlockSpec((8, tm), lambda i, ps: (0, i)),
            pl.BlockSpec(memory_space=pl.ANY),
        ],
        out_specs=[
            pl.BlockSpec((tm, D_MODEL), lambda i, ps: (jnp.minimum(i, nt0 - 1), 0)),
            pl.BlockSpec((tm, D_MODEL), lambda i, ps: (jnp.maximum(i - nt0, 0), 0)),
        ],
        scratch_shapes=[
            pltpu.VMEM((2, 2, tm * ROW_TILES, LANES), F32),
            pltpu.SemaphoreType.DMA((2,)),
        ],
    )
    return pl.pallas_call(
        functools.partial(_combine_kernel, nt0=nt0, n_tok=t),
        grid_spec=grid_spec,
        out_shape=[
            jax.ShapeDtypeStruct((t0, D_MODEL), F32),
            jax.ShapeDtypeStruct((t - t0, D_MODEL), F32),
        ],
        compiler_params=_cparams(("arbitrary",)),
        name="moe_combine",
    )(pos, xnew, wts, y_sorted)


def _rope_tables(max_len):
    inv = 1.0 / (ROPE_THETA ** (jnp.arange(0, HEAD_DIM, 2, dtype=F32) / HEAD_DIM))
    ang = jnp.arange(max_len, dtype=F32)[:, None] * inv[None, :]
    ang = jnp.concatenate([ang, ang, ang, ang], axis=-1)
    lane = jnp.arange(LANES)
    sign = jnp.where((lane % HEAD_DIM) < HEAD_DIM // 2, -1.0, 1.0).astype(F32)
    return jnp.cos(ang), jnp.sin(ang) * sign[None, :]


def _tile_tables(groups, tm):
    pos, length = [], []
    for batch, s_len in groups:
        for _ in range(batch):
            for p in range(0, s_len, tm):
                pos.append(p)
                length.append(s_len)
    return np.asarray(pos, np.int32), np.asarray(length, np.int32)


def _routing_tables(ids, n_tok, tm):
    e_flat = ids[0:2, :].reshape(-1)
    onehot = (e_flat[:, None] == jnp.arange(N_EXPERTS, dtype=I32)[None, :]).astype(I32)
    csum = jnp.cumsum(onehot, axis=0)
    rank = jnp.sum(csum * onehot, axis=1) - 1
    counts = csum[-1]
    padded = ((counts + tm - 1) // tm) * tm
    ends = jnp.cumsum(padded)
    starts = ends - padded
    dst = (starts[e_flat] + rank).astype(I32)
    p_rows = 2 * n_tok + N_EXPERTS * tm
    n_tiles = p_rows // tm
    tok = jnp.tile(jnp.arange(n_tok, dtype=I32), 2)
    src_row = jnp.zeros((p_rows,), I32).at[dst].set(tok)
    n_used = (ends[-1] // tm).astype(I32)
    tile_start = jnp.arange(n_tiles, dtype=I32) * tm
    tile_e = jnp.sum((ends[None, :] <= tile_start[:, None]).astype(I32), axis=1)
    tile_e = jnp.minimum(tile_e, N_EXPERTS - 1)
    active = jnp.arange(n_tiles, dtype=I32) < n_used
    last_e = tile_e[jnp.maximum(n_used - 1, 0)]
    tile_e = jnp.where(active, tile_e, last_e)
    tile_first = (active & (tile_start == starts[tile_e])).astype(I32)
    return tile_e, tile_first, n_used.reshape(1), src_row, dst


def _forward(x0, x1, groups, params, *, tm_proj, tn_proj, rc_proj, tq, tk, tm_post, tm_moe, tm_comb):
    (attn_norm_g, w_in, q_norm_g, k_norm_g, lambda_q1, lambda_k1, lambda_q2, lambda_k2, subln_g,
     w_attn_up, w_pool_grp, pool_scale, w_pool_up, w_out, ffn_norm_g, w_group_router,
     b_group_router, w_expert_router, b_expert_router, w_gate, w_up, w_down) = params
    t0, t1 = x0.shape[0], x1.shape[0]
    n_tok = t0 + t1
    lam_init = 0.8 - 0.6 * math.exp(-0.3 * 0)
    max_len = max(s for _, s in groups)

    cos_t, sin_t = _rope_tables(max_len)
    pos_proj, _ = _tile_tables(groups, tm_proj)
    q_scale = HEAD_DIM ** -0.5 * math.log2(math.e)
    qkg = jnp.zeros((8, ATTN_W), F32)
    qkg = qkg.at[0].set(jnp.tile(q_norm_g, ATTN_W // HEAD_DIM) * q_scale)
    qkg = qkg.at[1].set(jnp.tile(k_norm_g, ATTN_W // HEAD_DIM))
    seg = np.arange(LANES) // HEAD_DIM
    bd = jnp.asarray((seg[:, None] == seg[None, :]).astype(np.float32) / HEAD_DIM, BF16)

    proj = _in_projection(x0, x1, jnp.asarray(pos_proj // tm_proj), attn_norm_g[None, :],
                          w_in.astype(BF16), cos_t, sin_t, qkg, bd, tm=tm_proj, tn=tn_proj, rc=rc_proj)

    lam_pack = jnp.zeros((8, HEAD_DIM), F32)
    lam_pack = lam_pack.at[0].set(lambda_q1).at[1].set(lambda_k1).at[2].set(lambda_q2).at[3].set(lambda_k2)
    outs = []
    tok_off = 0
    for batch, s_len in groups:
        outs.append(_attention(proj, tok_off, batch, s_len, lam_pack, subln_g[None, :],
                               tq=min(tq, s_len), tk=min(tk, s_len), lam_init=lam_init))
        tok_off += batch * s_len

    tile_pos, tile_len = _tile_tables(groups, tm_post)
    w_router = jnp.zeros((D_MODEL, ROUTER_COLS), F32)
    w_router = w_router.at[:, 0:N_GROUPS].set(w_group_router)
    w_router = w_router.at[:, EXPERT_COL0:EXPERT_COL0 + N_EXPERTS].set(w_expert_router)
    b_router = jnp.zeros((1, ROUTER_COLS), F32)
    b_router = b_router.at[0, 0:N_GROUPS].set(b_group_router)
    b_router = b_router.at[0, EXPERT_COL0:EXPERT_COL0 + N_EXPERTS].set(b_expert_router)
    wr_hi = w_router.astype(BF16)
    wr_lo = (w_router - wr_hi.astype(F32)).astype(BF16)
    xnew, ids, wts = _post_block(
        x0, x1, outs[0], outs[1], proj, jnp.asarray(tile_pos), jnp.asarray(tile_len),
        w_attn_up.astype(BF16), w_pool_grp.astype(BF16), pool_scale[None, :], w_pool_up.astype(BF16),
        w_out.astype(BF16), ffn_norm_g[None, :], jnp.concatenate([wr_hi, wr_lo], axis=1), b_router,
        tm=tm_post)

    tile_e, tile_first, n_used, src_row, dst = _routing_tables(ids, n_tok, tm_moe)
    y_sorted = _moe_ffn(
        xnew, tile_e, tile_first, n_used, src_row, ffn_norm_g[None, :],
        w_gate.reshape(N_EXPERTS, D_MODEL, D_EXPERT), w_up.reshape(N_EXPERTS, D_MODEL, D_EXPERT),
        w_down.reshape(N_EXPERTS, D_EXPERT, D_MODEL), tm=tm_moe)
    return _combine(xnew, y_sorted, wts, dst, t0, tm=tm_comb)


def kernel(x_prompt, x_sample, attn_norm_g, w_in, q_norm_g, k_norm_g, lambda_q1, lambda_k1, lambda_q2,
           lambda_k2, subln_g, w_attn_up, w_pool_grp, pool_scale, w_pool_up, w_out, ffn_norm_g,
           w_group_router, b_group_router, w_expert_router, b_expert_router, w_gate, w_up, w_down):
    assert attn_norm_g.shape[0] == 1, "single-layer stack"
    params = tuple(p[0] for p in (
        attn_norm_g, w_in, q_norm_g, k_norm_g, lambda_q1, lambda_k1, lambda_q2, lambda_k2, subln_g,
        w_attn_up, w_pool_grp, pool_scale, w_pool_up, w_out, ffn_norm_g, w_group_router,
        b_group_router, w_expert_router, b_expert_router, w_gate, w_up, w_down))
    groups = (x_prompt.shape[:2], x_sample.shape[:2])
    y0, y1 = _forward(
        x_prompt.reshape(-1, D_MODEL), x_sample.reshape(-1, D_MODEL), groups, params,
        tm_proj=1024, tn_proj=1024, rc_proj=512, tq=1024, tk=512, tm_post=256, tm_moe=256, tm_comb=256)
    return y0.reshape(x_prompt.shape), y1.reshape(x_sample.shape)
```

```python
import functools
import math

import numpy as np
import jax
import jax.numpy as jnp
from jax import lax
from jax.experimental import pallas as pl
from jax.experimental.pallas import tpu as pltpu

F32 = jnp.float32
BF16 = jnp.bfloat16
I32 = jnp.int32

D_MODEL = 2048
HEAD_DIM = 64
N_HEADS = 8
HEAD_W = 2 * HEAD_DIM
ATTN_W = N_HEADS * HEAD_W
POOL_W = 1024
POOL_WINDOWS = (2, 4, 8, 16)
POOL_GROUP_W = POOL_W // len(POOL_WINDOWS)
POOL_HALO = 64
IN_COLS = 3 * ATTN_W + POOL_W + 2 * D_MODEL
N_GROUPS = 4
EXPERTS_PER_GROUP = 8
N_EXPERTS = N_GROUPS * EXPERTS_PER_GROUP
D_EXPERT = 512
ROPE_THETA = 10000.0
EPS = 1e-6
LANES = 128
ROUTER_COLS = 128
EXPERT_COL0 = 8
ROW_TILES = D_MODEL // LANES
ROW_PITCH = ROW_TILES + 1
SAFE_SCORE_BOUND = 50.0
SCORE_BOUND_MARGIN = 1.02

COL_Q, COL_K, COL_V, COL_PIN, COL_GA, COL_GP = 0, 1, 2, 3, 4, 6

VMEM_LIMIT = 56 * 1024 * 1024


def _cparams(sem, vmem=VMEM_LIMIT):
    return pltpu.CompilerParams(dimension_semantics=sem, vmem_limit_bytes=vmem)


def _token_slab(ref, n_tok, s):
    return ref[pl.ds(s, n_tok, stride=ROW_PITCH), :]


def _load_token_rows(ref, n_tok):
    return jnp.concatenate([_token_slab(ref, n_tok, s) for s in range(ROW_TILES)], axis=1)


def _store_token_rows(ref, value):
    n_tok = value.shape[0]
    for s in range(ROW_TILES):
        ref[pl.ds(s, n_tok, stride=ROW_PITCH), :] = value[:, s * LANES:(s + 1) * LANES]
    ref[pl.ds(ROW_TILES, n_tok, stride=ROW_PITCH), :] = jnp.zeros((n_tok, LANES), ref.dtype)


def _inproj_kernel(posblk_ref, x0_ref, x1_ref, gattn_ref, w_ref, cos_ref, sin_ref, qkg_ref, bd_ref,
                   o_ref, xn_ref, *, nt0, rc):
    del posblk_ref
    i = pl.program_id(0)
    j = pl.program_id(1)
    tm = xn_ref.shape[0]
    n_chunks = tm // rc

    def normalise(x_ref):
        for c in range(n_chunks):
            xs = x_ref[c * rc:(c + 1) * rc, :]
            ms = jnp.mean(xs * xs, axis=-1, keepdims=True)
            xn_ref[c * rc:(c + 1) * rc, :] = (xs * lax.rsqrt(ms + EPS) * gattn_ref[...]).astype(BF16)

    @pl.when((j == 0) & (i < nt0))
    def _():
        normalise(x0_ref)

    @pl.when((j == 0) & (i >= nt0))
    def _():
        normalise(x1_ref)

    def chunk_acc(c):
        return jnp.dot(xn_ref[c * rc:(c + 1) * rc, :], w_ref[...], preferred_element_type=F32)

    @pl.when(j <= COL_K)
    def _():
        g = jnp.where(j == COL_Q, qkg_ref[0:1, :], qkg_ref[1:2, :])
        lane = lax.broadcasted_iota(I32, (rc, LANES), 1)
        first_half = (lane % HEAD_DIM) < (HEAD_DIM // 2)
        for c in range(n_chunks):
            a = chunk_acc(c)
            cos = cos_ref[c * rc:(c + 1) * rc, :]
            sin = sin_ref[c * rc:(c + 1) * rc, :]
            for t in range(a.shape[1] // LANES):
                at = a[:, t * LANES:(t + 1) * LANES]
                ms = jnp.dot((at * at).astype(BF16), bd_ref[...], preferred_element_type=F32)
                u = at * g[:, t * LANES:(t + 1) * LANES]
                rot = jnp.where(first_half, pltpu.roll(u, LANES - HEAD_DIM // 2, 1),
                                pltpu.roll(u, HEAD_DIM // 2, 1))
                val = (u * cos + rot * sin) * lax.rsqrt(ms + EPS)
                o_ref[c * rc:(c + 1) * rc, t * LANES:(t + 1) * LANES] = val.astype(BF16)

    @pl.when((j == COL_V) | (j == COL_PIN))
    def _():
        for c in range(n_chunks):
            o_ref[c * rc:(c + 1) * rc, :] = chunk_acc(c).astype(BF16)

    @pl.when(j >= COL_GA)
    def _():
        for c in range(n_chunks):
            o_ref[c * rc:(c + 1) * rc, :] = jax.nn.sigmoid(chunk_acc(c)).astype(BF16)


def _in_projection(x0, x1, posblk, gattn, w_in, cos_t, sin_t, qkg, bd, *, tm, tn, rc):
    t0, t1 = x0.shape[0], x1.shape[0]
    nt0, nt1 = t0 // tm, t1 // tm
    nt = nt0 + nt1
    grid_spec = pltpu.PrefetchScalarGridSpec(
        num_scalar_prefetch=1,
        grid=(nt, IN_COLS // tn),
        in_specs=[
            pl.BlockSpec((tm, D_MODEL), lambda i, j, pb: (jnp.minimum(i, nt0 - 1), 0),
                         pipeline_mode=pl.Buffered(1)),
            pl.BlockSpec((tm, D_MODEL), lambda i, j, pb: (jnp.maximum(i - nt0, 0), 0),
                         pipeline_mode=pl.Buffered(1)),
            pl.BlockSpec((1, D_MODEL), lambda i, j, pb: (0, 0)),
            pl.BlockSpec((D_MODEL, tn), lambda i, j, pb: (0, j)),
            pl.BlockSpec((tm, LANES), lambda i, j, pb: (pb[i], 0)),
            pl.BlockSpec((tm, LANES), lambda i, j, pb: (pb[i], 0)),
            pl.BlockSpec((8, tn), lambda i, j, pb: (0, 0)),
            pl.BlockSpec((LANES, LANES), lambda i, j, pb: (0, 0)),
        ],
        out_specs=pl.BlockSpec((tm, tn), lambda i, j, pb: (i, j)),
        scratch_shapes=[pltpu.VMEM((tm, D_MODEL), BF16)],
    )
    return pl.pallas_call(
        functools.partial(_inproj_kernel, nt0=nt0, rc=rc),
        grid_spec=grid_spec,
        out_shape=jax.ShapeDtypeStruct((t0 + t1, IN_COLS), BF16),
        compiler_params=_cparams(("arbitrary", "arbitrary")),
        name="in_projection",
    )(posblk, x0, x1, gattn, w_in, cos_t, sin_t, qkg, bd)


def _attn_kernel(q_ref, k_ref, v_ref, lam_ref, sg_ref, o_ref, vt_ref, acc_ref, m_ref, l_ref, kn_ref,
                 p_ref, *, tk, lam_init):
    i = pl.program_id(2)
    s_len = k_ref.shape[0]
    tq = q_ref.shape[0]
    nk = s_len // tk
    nt_dims = (((1,), (1,)), ((), ()))

    sel_r = lax.broadcasted_iota(I32, (8, HEAD_W), 0)
    sel_l = lax.broadcasted_iota(I32, (8, HEAD_W), 1)
    sel = jnp.where(sel_l // HEAD_DIM == sel_r, 1.0, 0.0).astype(BF16)

    @pl.when(i == 0)
    def _():
        kn2 = jnp.zeros((8, 1), F32)
        for jb in range(nk):
            rows = slice(jb * tk, (jb + 1) * tk)
            vt_ref[:, rows] = v_ref[rows, :].astype(F32).T.astype(BF16)
            kf = k_ref[rows, :].astype(F32)
            n2 = lax.dot_general(sel, (kf * kf).astype(BF16), nt_dims, preferred_element_type=F32)
            kn2 = jnp.maximum(kn2, jnp.max(n2, axis=1, keepdims=True))
        kn_ref[...] = jnp.broadcast_to(kn2, kn_ref.shape)

    q = q_ref[...]
    lane = lax.broadcasted_iota(I32, q.shape, 1)
    zero = jnp.zeros_like(q)
    qc = (jnp.where(lane < HEAD_DIM, q, zero), jnp.where(lane >= HEAD_DIM, q, zero))

    qf = q.astype(F32)
    qn2 = lax.dot_general(sel, (qf * qf).astype(BF16), nt_dims, preferred_element_type=F32)
    bound = jnp.sqrt(qn2 * kn_ref[:, 0:1]) * SCORE_BOUND_MARGIN
    bound_c = (bound[0:1, :], bound[1:2, :])
    use_bound = jnp.max(bound[0:2, :]) <= SAFE_SCORE_BOUND

    l_ref[...] = jnp.zeros(l_ref.shape, F32)
    acc_ref[...] = jnp.zeros(acc_ref.shape, F32)

    def load_blocks(jb):
        off = jb * tk if isinstance(jb, int) else pl.multiple_of(jb * tk, tk)
        return k_ref[pl.ds(off, tk), :], vt_ref[:, pl.ds(off, tk)]

    def bounded_weights(jb, slot):
        kb, _ = load_blocks(jb)
        for c in range(2):
            st = lax.dot_general(kb, qc[c], nt_dims, preferred_element_type=F32)
            p = jnp.exp2(st - bound_c[c])
            l_ref[c] += jnp.sum(p, axis=0, keepdims=True)
            p_ref[slot, c] = p.astype(BF16)

    def weighted_values(jb, slot):
        _, vtb = load_blocks(jb)
        for c in range(2):
            acc_ref[c] += jnp.dot(vtb, p_ref[slot, c], preferred_element_type=F32)

    def bounded_pair(m, carry):
        j0 = 2 * m
        bounded_weights(j0 + 1, 1)
        weighted_values(j0, 0)
        bounded_weights(j0 + 2, 0)
        weighted_values(j0 + 1, 1)
        return carry

    def running_max_step(jb, carry):
        kb, vtb = load_blocks(jb)
        for c in range(2):
            st = lax.dot_general(kb, qc[c], nt_dims, preferred_element_type=F32)
            m_old = m_ref[c]
            m_new = jnp.maximum(m_old, jnp.max(st, axis=0, keepdims=True))
            alpha = jnp.exp2(m_old - m_new)
            p = jnp.exp2(st - m_new)
            l_ref[c] = alpha * l_ref[c] + jnp.sum(p, axis=0, keepdims=True)
            acc_ref[c] = alpha * acc_ref[c] + jnp.dot(vtb, p.astype(BF16), preferred_element_type=F32)
            m_ref[c] = m_new
        return carry

    @pl.when(use_bound)
    def _():
        bounded_weights(0, 0)
        lax.fori_loop(0, nk // 2 - 1, bounded_pair, 0)
        bounded_weights(nk - 1, 1)
        weighted_values(nk - 2, 0)
        weighted_values(nk - 1, 1)

    @pl.when(jnp.logical_not(use_bound))
    def _():
        m_ref[...] = jnp.full(m_ref.shape, -jnp.inf, F32)
        lax.fori_loop(0, nk, running_max_step, 0)

    lq1, lk1, lq2, lk2 = lam_ref[0:1, :], lam_ref[1:2, :], lam_ref[2:3, :], lam_ref[3:4, :]
    lam = (jnp.exp(jnp.sum(lq1 * lk1, axis=-1, keepdims=True))
           - jnp.exp(jnp.sum(lq2 * lk2, axis=-1, keepdims=True)) + lam_init)
    ot = acc_ref[0] / l_ref[0] - lam * (acc_ref[1] / l_ref[1])
    ms = jnp.mean(ot * ot, axis=0, keepdims=True)
    on = (ot * lax.rsqrt(ms + EPS)).T
    o_ref[...] = (on * sg_ref[...] * (1.0 - lam_init)).astype(BF16)


def _attention(proj, tok_off, batch, s_len, lam_pack, subln_g, *, tq, tk, lam_init):
    qb0 = tok_off // tq
    sb0 = tok_off // s_len
    nq = s_len // tq
    return pl.pallas_call(
        functools.partial(_attn_kernel, tk=tk, lam_init=lam_init),
        grid=(batch, N_HEADS, nq),
        in_specs=[
            pl.BlockSpec((tq, HEAD_W), lambda b, h, i: (qb0 + b * nq + i, COL_Q * N_HEADS + h)),
            pl.BlockSpec((s_len, HEAD_W), lambda b, h, i: (sb0 + b, COL_K * N_HEADS + h)),
            pl.BlockSpec((s_len, HEAD_W), lambda b, h, i: (sb0 + b, COL_V * N_HEADS + h)),
            pl.BlockSpec((8, HEAD_DIM), lambda b, h, i: (0, 0)),
            pl.BlockSpec((1, HEAD_W), lambda b, h, i: (0, 0)),
        ],
        out_specs=pl.BlockSpec((tq, HEAD_W), lambda b, h, i: (b * nq + i, h)),
        out_shape=jax.ShapeDtypeStruct((batch * s_len, ATTN_W), BF16),
        scratch_shapes=[
            pltpu.VMEM((HEAD_W, s_len), BF16),
            pltpu.VMEM((2, HEAD_W, tq), F32),
            pltpu.VMEM((2, 1, tq), F32),
            pltpu.VMEM((2, 1, tq), F32),
            pltpu.VMEM((8, LANES), F32),
            pltpu.VMEM((2, 2, tk, tq), BF16),
        ],
        compiler_params=_cparams(("arbitrary", "arbitrary", "arbitrary")),
        name="diff_attention",
    )(proj, proj, proj, lam_pack, subln_g)


def _post_kernel(pos_ref, len_ref, x0_ref, x1_ref, o0_ref, o1_ref, pin_ref, prev_ref, next_ref,
                 ga_ref, gp_ref, wau_ref, wgrp_ref, pscale_ref, wpu_ref, wout_ref, gffn_ref,
                 wrc_ref, br_ref, xnew_ref, ids_ref, wts_ref, *, nt0):
    i = pl.program_id(0)
    tm = pin_ref.shape[0]
    p0 = pos_ref[i]
    s_len = len_ref[i]
    first = i < nt0

    o = jnp.where(first, o0_ref[...], o1_ref[...])
    attn_d = jnp.dot(o, wau_ref[...], preferred_element_type=F32)

    ext = jnp.concatenate([prev_ref[...], pin_ref[...], next_ref[...]], axis=0)
    ke = tm + 2 * POOL_HALO
    r_band = lax.broadcasted_iota(I32, (tm, ke), 0) + p0
    c_band = lax.broadcasted_iota(I32, (tm, ke), 1) + (p0 - POOL_HALO)
    r_cnt = lax.broadcasted_iota(I32, (tm, POOL_GROUP_W), 0) + p0
    pooled = []
    for gi, w in enumerate(POOL_WINDOWS):
        back, fwd = w // 2, w - 1 - w // 2
        lo = jnp.maximum(r_band - back, 0)
        hi = jnp.minimum(r_band + fwd, s_len - 1)
        band = jnp.where((c_band >= lo) & (c_band <= hi), 1.0, 0.0).astype(BF16)
        cnt = jnp.minimum(r_cnt + fwd, s_len - 1) - jnp.maximum(r_cnt - back, 0) + 1
        cols = slice(gi * POOL_GROUP_W, (gi + 1) * POOL_GROUP_W)
        wsum = jnp.dot(band, ext[:, cols], preferred_element_type=F32)
        y = wsum / cnt.astype(F32) - pin_ref[:, cols].astype(F32)
        pooled.append(jnp.dot(y.astype(BF16), wgrp_ref[gi], preferred_element_type=F32))
    pooled = (jnp.concatenate(pooled, axis=1) * pscale_ref[...]).astype(BF16)
    pool_d = jnp.dot(pooled, wpu_ref[...], preferred_element_type=F32)

    merged = ga_ref[...].astype(F32) * attn_d + gp_ref[...].astype(F32) * pool_d
    x = jnp.where(first, x0_ref[...], x1_ref[...])
    xnew = x + jnp.dot(merged.astype(BF16), wout_ref[...], preferred_element_type=F32)
    _store_token_rows(xnew_ref, xnew)

    ms = jnp.mean(xnew * xnew, axis=-1, keepdims=True)
    hn = xnew * lax.rsqrt(ms + EPS) * gffn_ref[...]
    hi_part = hn.astype(BF16)
    lo_part = (hn - hi_part.astype(F32)).astype(BF16)
    hw = jnp.dot(hi_part, wrc_ref[...], preferred_element_type=F32)
    logits = (hw[:, 0:ROUTER_COLS]
              + (hw[:, ROUTER_COLS:]
                 + jnp.dot(lo_part, wrc_ref[:, 0:ROUTER_COLS], preferred_element_type=F32))
              + br_ref[...])
    lt = logits.T

    gl = lt[0:N_GROUPS, :]
    g_iota = lax.broadcasted_iota(I32, gl.shape, 0)
    g_max = jnp.max(gl, axis=0, keepdims=True)
    g_sel = jnp.min(jnp.where(gl == g_max, g_iota, N_GROUPS), axis=0, keepdims=True)
    g_w = 1.0 / jnp.sum(jnp.exp(gl - g_max), axis=0, keepdims=True)
    el = lt[EXPERT_COL0:EXPERT_COL0 + N_EXPERTS, :]
    e_iota = lax.broadcasted_iota(I32, el.shape, 0)
    neg = jnp.float32(-jnp.inf)
    el = jnp.where(e_iota // EXPERTS_PER_GROUP == g_sel, el, neg)
    v1 = jnp.max(el, axis=0, keepdims=True)
    i1 = jnp.min(jnp.where(el == v1, e_iota, N_EXPERTS), axis=0, keepdims=True)
    el2 = jnp.where(e_iota == i1, neg, el)
    v2 = jnp.max(el2, axis=0, keepdims=True)
    i2 = jnp.min(jnp.where(el2 == v2, e_iota, N_EXPERTS), axis=0, keepdims=True)
    d = jnp.exp(v2 - v1)
    w1 = g_w / (1.0 + d)
    w2 = g_w * d / (1.0 + d)
    row = lax.broadcasted_iota(I32, (8, tm), 0)
    ids_ref[...] = jnp.where(row == 0, i1, jnp.where(row == 1, i2, 0))
    wts_ref[...] = jnp.where(row == 0, w1, jnp.where(row == 1, w2, 0.0))


def _post_block(x0, x1, o0, o1, proj, tile_pos, tile_len, wau, wgrp, pscale, wpu, wout, gffn,
                wrc, br, *, tm):
    t0, t1 = x0.shape[0], x1.shape[0]
    t = t0 + t1
    nt0 = t0 // tm
    nt = t // tm
    hb = tm // POOL_HALO
    n_hblk = t // POOL_HALO
    const = lambda *shape: pl.BlockSpec(shape, lambda i, tp, tl: (0,) * len(shape),
                                        pipeline_mode=pl.Buffered(1))
    grid_spec = pltpu.PrefetchScalarGridSpec(
        num_scalar_prefetch=2,
        grid=(nt,),
        in_specs=[
            pl.BlockSpec((tm, D_MODEL), lambda i, tp, tl: (jnp.minimum(i, nt0 - 1), 0)),
            pl.BlockSpec((tm, D_MODEL), lambda i, tp, tl: (jnp.maximum(i - nt0, 0), 0)),
            pl.BlockSpec((tm, ATTN_W), lambda i, tp, tl: (jnp.minimum(i, nt0 - 1), 0)),
            pl.BlockSpec((tm, ATTN_W), lambda i, tp, tl: (jnp.maximum(i - nt0, 0), 0)),
            pl.BlockSpec((tm, POOL_W), lambda i, tp, tl: (i, COL_PIN)),
            pl.BlockSpec((POOL_HALO, POOL_W), lambda i, tp, tl: (jnp.maximum(i * hb - 1, 0), COL_PIN)),
            pl.BlockSpec((POOL_HALO, POOL_W), lambda i, tp, tl: (jnp.minimum((i + 1) * hb, n_hblk - 1), COL_PIN)),
            pl.BlockSpec((tm, D_MODEL), lambda i, tp, tl: (i, COL_GA // 2)),
            pl.BlockSpec((tm, D_MODEL), lambda i, tp, tl: (i, COL_GP // 2)),
            const(ATTN_W, D_MODEL),
            const(len(POOL_WINDOWS), POOL_GROUP_W, POOL_GROUP_W),
            const(1, POOL_W),
            const(POOL_W, D_MODEL),
            const(D_MODEL, D_MODEL),
            const(1, D_MODEL),
            const(D_MODEL, 2 * ROUTER_COLS),
            const(1, ROUTER_COLS),
        ],
        out_specs=[
            pl.BlockSpec((tm * ROW_PITCH, LANES), lambda i, tp, tl: (i, 0)),
            pl.BlockSpec((8, tm), lambda i, tp, tl: (0, i)),
            pl.BlockSpec((8, tm), lambda i, tp, tl: (0, i)),
        ],
    )
    return pl.pallas_call(
        functools.partial(_post_kernel, nt0=nt0),
        grid_spec=grid_spec,
        out_shape=[
            jax.ShapeDtypeStruct((t * ROW_PITCH, LANES), F32),
            jax.ShapeDtypeStruct((8, t), I32),
            jax.ShapeDtypeStruct((8, t), F32),
        ],
        compiler_params=_cparams(("arbitrary",)),
        name="post_block",
    )(tile_pos, tile_len, x0, x1, o0, o1, proj, proj, proj, proj, proj, wau, wgrp, pscale, wpu,
      wout, gffn, wrc, br)


GATHER_UNROLL = 8


def _row_gather_start(src_hbm, idx_ref, base, n_rows, buf, sem):
    def body(r, carry):
        src0 = idx_ref[base + r] * ROW_PITCH
        dst0 = r * ROW_PITCH
        pltpu.make_async_copy(src_hbm.at[pl.ds(src0, ROW_TILES)], buf.at[pl.ds(dst0, ROW_TILES)], sem).start()
        return carry
    lax.fori_loop(0, n_rows, body, 0, unroll=GATHER_UNROLL)


def _row_gather_wait(src_hbm, n_rows, buf, sem):
    n = n_rows * ROW_TILES
    pltpu.make_async_copy(src_hbm.at[pl.ds(0, n)], buf.at[pl.ds(0, n)], sem).wait()


def _moe_kernel(te_ref, tf_ref, nu_ref, src_ref, x_hbm, gffn_ref, wg_ref, wu_ref, wd_ref,
                y_ref, xbuf, sem, wgb, wub, wdb):
    del te_ref
    i = pl.program_id(0)
    tm = y_ref.shape[0] // ROW_PITCH
    n_used = nu_ref[0]
    slot = lax.rem(i, 2)

    @pl.when(i == 0)
    def _():
        _row_gather_start(x_hbm, src_ref, 0, tm, xbuf.at[0], sem.at[0])

    @pl.when(i + 1 < n_used)
    def _():
        _row_gather_start(x_hbm, src_ref, (i + 1) * tm, tm, xbuf.at[1 - slot], sem.at[1 - slot])

    @pl.when(i < n_used)
    def _():
        @pl.when(tf_ref[i] == 1)
        def _():
            wgb[...] = wg_ref[0].astype(BF16)
            wub[...] = wu_ref[0].astype(BF16)
            wdb[...] = wd_ref[0].astype(BF16)

        _row_gather_wait(x_hbm, tm, xbuf.at[slot], sem.at[slot])
        xs = _load_token_rows(xbuf.at[slot], tm)
        ms = jnp.mean(xs * xs, axis=-1, keepdims=True)
        hn = (xs * lax.rsqrt(ms + EPS) * gffn_ref[...]).astype(BF16)
        gate = jnp.dot(hn, wgb[...], preferred_element_type=F32)
        up = jnp.dot(hn, wub[...], preferred_element_type=F32)
        h = jax.nn.silu(gate) * up
        _store_token_rows(y_ref, jnp.dot(h.astype(BF16), wdb[...], preferred_element_type=F32))

    @pl.when(i >= n_used)
    def _():
        y_ref[...] = jnp.zeros(y_ref.shape, y_ref.dtype)


def _moe_ffn(xnew, tile_e, tile_first, n_used, src_row, gffn, w_gate, w_up, w_down, *, tm):
    p_rows = src_row.shape[0]
    n_tiles = p_rows // tm
    grid_spec = pltpu.PrefetchScalarGridSpec(
        num_scalar_prefetch=4,
        grid=(n_tiles,),
        in_specs=[
            pl.BlockSpec(memory_space=pl.ANY),
            pl.BlockSpec((1, D_MODEL), lambda i, te, tf, nu, sr: (0, 0)),
            pl.BlockSpec((1, D_MODEL, D_EXPERT), lambda i, te, tf, nu, sr: (te[i], 0, 0)),
            pl.BlockSpec((1, D_MODEL, D_EXPERT), lambda i, te, tf, nu, sr: (te[i], 0, 0)),
            pl.BlockSpec((1, D_EXPERT, D_MODEL), lambda i, te, tf, nu, sr: (te[i], 0, 0)),
        ],
        out_specs=pl.BlockSpec((tm * ROW_PITCH, LANES), lambda i, te, tf, nu, sr: (i, 0)),
        scratch_shapes=[
            pltpu.VMEM((2, tm * ROW_PITCH, LANES), F32),
            pltpu.SemaphoreType.DMA((2,)),
            pltpu.VMEM((D_MODEL, D_EXPERT), BF16),
            pltpu.VMEM((D_MODEL, D_EXPERT), BF16),
            pltpu.VMEM((D_EXPERT, D_MODEL), BF16),
        ],
    )
    return pl.pallas_call(
        _moe_kernel,
        grid_spec=grid_spec,
        out_shape=jax.ShapeDtypeStruct((p_rows * ROW_PITCH, LANES), F32),
        compiler_params=_cparams(("arbitrary",)),
        name="moe_ffn",
    )(tile_e, tile_first, n_used, src_row, xnew, gffn, w_gate, w_up, w_down)


def _combine_kernel(pos_ref, xnew_ref, wts_ref, y_hbm, out0_ref, out1_ref, ybuf, sem, *, nt0, n_tok):
    i = pl.program_id(0)
    nt = pl.num_programs(0)
    tm = xnew_ref.shape[0] // ROW_PITCH
    slot = lax.rem(i, 2)

    def start(tile, s):
        _row_gather_start(y_hbm, pos_ref, tile * tm, tm, ybuf.at[s, 0], sem.at[s])
        _row_gather_start(y_hbm, pos_ref, n_tok + tile * tm, tm, ybuf.at[s, 1], sem.at[s])

    @pl.when(i == 0)
    def _():
        start(0, 0)

    @pl.when(i + 1 < nt)
    def _():
        start(i + 1, 1 - slot)

    _row_gather_wait(y_hbm, tm, ybuf.at[slot, 0], sem.at[slot])
    _row_gather_wait(y_hbm, tm, ybuf.at[slot, 1], sem.at[slot])
    wt = wts_ref[...].T
    w1 = jnp.broadcast_to(wt[:, 0:1], (tm, LANES))
    w2 = jnp.broadcast_to(wt[:, 1:2], (tm, LANES))

    def write(out_ref):
        for s in range(ROW_TILES):
            out_ref[:, s * LANES:(s + 1) * LANES] = (
                _token_slab(xnew_ref, tm, s)
                + (w1 * _token_slab(ybuf.at[slot, 0], tm, s) + w2 * _token_slab(ybuf.at[slot, 1], tm, s)))

    @pl.when(i < nt0)
    def _():
        write(out0_ref)

    @pl.when(i >= nt0)
    def _():
        write(out1_ref)


def _combine(xnew, y_sorted, wts, pos, t0, *, tm):
    t = xnew.shape[0] // ROW_PITCH
    nt0 = t0 // tm
    nt = t // tm
    grid_spec = pltpu.PrefetchScalarGridSpec(
        num_scalar_prefetch=1,
        grid=(nt,),
        in_specs=[
            pl.BlockSpec((tm * ROW_PITCH, LANES), lambda i, ps: (i, 0)),
            pl.BlockSpec((8, tm), lambda i, ps: (0, i)),
            pl.BlockSpec(memory_space=pl.ANY),
        ],
        out_specs=[
            pl.BlockSpec((tm, D_MODEL), lambda i, ps: (jnp.minimum(i, nt0 - 1), 0)),
            pl.BlockSpec((tm, D_MODEL), lambda i, ps: (jnp.maximum(i - nt0, 0), 0)),
        ],
        scratch_shapes=[
            pltpu.VMEM((2, 2, tm * ROW_PITCH, LANES), F32),
            pltpu.SemaphoreType.DMA((2,)),
        ],
    )
    return pl.pallas_call(
        functools.partial(_combine_kernel, nt0=nt0, n_tok=t),
        grid_spec=grid_spec,
        out_shape=[
            jax.ShapeDtypeStruct((t0, D_MODEL), F32),
            jax.ShapeDtypeStruct((t - t0, D_MODEL), F32),
        ],
        compiler_params=_cparams(("arbitrary",)),
        name="moe_combine",
    )(pos, xnew, wts, y_sorted)


def _rope_tables(max_len):
    inv = 1.0 / (ROPE_THETA ** (jnp.arange(0, HEAD_DIM, 2, dtype=F32) / HEAD_DIM))
    ang = jnp.arange(max_len, dtype=F32)[:, None] * inv[None, :]
    ang = jnp.concatenate([ang, ang, ang, ang], axis=-1)
    lane = jnp.arange(LANES)
    sign = jnp.where((lane % HEAD_DIM) < HEAD_DIM // 2, -1.0, 1.0).astype(F32)
    return jnp.cos(ang), jnp.sin(ang) * sign[None, :]


def _tile_tables(groups, tm):
    pos, length = [], []
    for batch, s_len in groups:
        for _ in range(batch):
            for p in range(0, s_len, tm):
                pos.append(p)
                length.append(s_len)
    return np.asarray(pos, np.int32), np.asarray(length, np.int32)


def _routing_tables(ids, n_tok, tm):
    e_flat = ids[0:2, :].reshape(-1)
    onehot = (e_flat[:, None] == jnp.arange(N_EXPERTS, dtype=I32)[None, :]).astype(I32)
    csum = jnp.cumsum(onehot, axis=0)
    rank = jnp.sum(csum * onehot, axis=1) - 1
    counts = csum[-1]
    padded = ((counts + tm - 1) // tm) * tm
    ends = jnp.cumsum(padded)
    starts = ends - padded
    dst = (starts[e_flat] + rank).astype(I32)
    p_rows = 2 * n_tok + N_EXPERTS * tm
    n_tiles = p_rows // tm
    tok = jnp.tile(jnp.arange(n_tok, dtype=I32), 2)
    src_row = jnp.zeros((p_rows,), I32).at[dst].set(tok)
    n_used = (ends[-1] // tm).astype(I32)
    tile_start = jnp.arange(n_tiles, dtype=I32) * tm
    tile_e = jnp.sum((ends[None, :] <= tile_start[:, None]).astype(I32), axis=1)
    tile_e = jnp.minimum(tile_e, N_EXPERTS - 1)
    active = jnp.arange(n_tiles, dtype=I32) < n_used
    last_e = tile_e[jnp.maximum(n_used - 1, 0)]
    tile_e = jnp.where(active, tile_e, last_e)
    tile_first = (active & (tile_start == starts[tile_e])).astype(I32)
    return tile_e, tile_first, n_used.reshape(1), src_row, dst


def _forward(x0, x1, groups, params, *, tm_proj, tn_proj, rc_proj, tq, tk, tm_post, tm_moe, tm_comb):
    (attn_norm_g, w_in, q_norm_g, k_norm_g, lambda_q1, lambda_k1, lambda_q2, lambda_k2, subln_g,
     w_attn_up, w_pool_grp, pool_scale, w_pool_up, w_out, ffn_norm_g, w_group_router,
     b_group_router, w_expert_router, b_expert_router, w_gate, w_up, w_down) = params
    t0, t1 = x0.shape[0], x1.shape[0]
    n_tok = t0 + t1
    lam_init = 0.8 - 0.6 * math.exp(-0.3 * 0)
    max_len = max(s for _, s in groups)

    cos_t, sin_t = _rope_tables(max_len)
    pos_proj, _ = _tile_tables(groups, tm_proj)
    q_scale = HEAD_DIM ** -0.5 * math.log2(math.e)
    qkg = jnp.zeros((8, ATTN_W), F32)
    qkg = qkg.at[0].set(jnp.tile(q_norm_g, ATTN_W // HEAD_DIM) * q_scale)
    qkg = qkg.at[1].set(jnp.tile(k_norm_g, ATTN_W // HEAD_DIM))
    seg = np.arange(LANES) // HEAD_DIM
    bd = jnp.asarray((seg[:, None] == seg[None, :]).astype(np.float32) / HEAD_DIM, BF16)

    proj = _in_projection(x0, x1, jnp.asarray(pos_proj // tm_proj), attn_norm_g[None, :],
                          w_in.astype(BF16), cos_t, sin_t, qkg, bd, tm=tm_proj, tn=tn_proj, rc=rc_proj)

    lam_pack = jnp.zeros((8, HEAD_DIM), F32)
    lam_pack = lam_pack.at[0].set(lambda_q1).at[1].set(lambda_k1).at[2].set(lambda_q2).at[3].set(lambda_k2)
    outs = []
    tok_off = 0
    for batch, s_len in groups:
        outs.append(_attention(proj, tok_off, batch, s_len, lam_pack, subln_g[None, :],
                               tq=min(tq, s_len), tk=min(tk, s_len), lam_init=lam_init))
        tok_off += batch * s_len

    tile_pos, tile_len = _tile_tables(groups, tm_post)
    w_router = jnp.zeros((D_MODEL, ROUTER_COLS), F32)
    w_router = w_router.at[:, 0:N_GROUPS].set(w_group_router)
    w_router = w_router.at[:, EXPERT_COL0:EXPERT_COL0 + N_EXPERTS].set(w_expert_router)
    b_router = jnp.zeros((1, ROUTER_COLS), F32)
    b_router = b_router.at[0, 0:N_GROUPS].set(b_group_router)
    b_router = b_router.at[0, EXPERT_COL0:EXPERT_COL0 + N_EXPERTS].set(b_expert_router)
    wr_hi = w_router.astype(BF16)
    wr_lo = (w_router - wr_hi.astype(F32)).astype(BF16)
    xnew, ids, wts = _post_block(
        x0, x1, outs[0], outs[1], proj, jnp.asarray(tile_pos), jnp.asarray(tile_len),
        w_attn_up.astype(BF16), w_pool_grp.astype(BF16), pool_scale[None, :], w_pool_up.astype(BF16),
        w_out.astype(BF16), ffn_norm_g[None, :], jnp.concatenate([wr_hi, wr_lo], axis=1), b_router,
        tm=tm_post)

    tile_e, tile_first, n_used, src_row, dst = _routing_tables(ids, n_tok, tm_moe)
    y_sorted = _moe_ffn(
        xnew, tile_e, tile_first, n_used, src_row, ffn_norm_g[None, :],
        w_gate.reshape(N_EXPERTS, D_MODEL, D_EXPERT), w_up.reshape(N_EXPERTS, D_MODEL, D_EXPERT),
        w_down.reshape(N_EXPERTS, D_EXPERT, D_MODEL), tm=tm_moe)
    return _combine(xnew, y_sorted, wts, dst, t0, tm=tm_comb)


def kernel(x_prompt, x_sample, attn_norm_g, w_in, q_norm_g, k_norm_g, lambda_q1, lambda_k1, lambda_q2,
           lambda_k2, subln_g, w_attn_up, w_pool_grp, pool_scale, w_pool_up, w_out, ffn_norm_g,
           w_group_router, b_group_router, w_expert_router, b_expert_router, w_gate, w_up, w_down):
    assert attn_norm_g.shape[0] == 1, "single-layer stack"
    params = tuple(p[0] for p in (
        attn_norm_g, w_in, q_norm_g, k_norm_g, lambda_q1, lambda_k1, lambda_q2, lambda_k2, subln_g,
        w_attn_up, w_pool_grp, pool_scale, w_pool_up, w_out, ffn_norm_g, w_group_router,
        b_group_router, w_expert_router, b_expert_router, w_gate, w_up, w_down))
    groups = (x_prompt.shape[:2], x_sample.shape[:2])
    y0, y1 = _forward(
        x_prompt.reshape(-1, D_MODEL), x_sample.reshape(-1, D_MODEL), groups, params,
        tm_proj=1024, tn_proj=1024, rc_proj=512, tq=1024, tk=512, tm_post=256, tm_moe=256, tm_comb=256)
    return y0.reshape(x_prompt.shape), y1.reshape(x_sample.shape)
```

```python
import functools
import math

import numpy as np
import jax
import jax.numpy as jnp
from jax import lax
from jax.experimental import pallas as pl
from jax.experimental.pallas import tpu as pltpu

F32 = jnp.float32
BF16 = jnp.bfloat16
I32 = jnp.int32

D_MODEL = 2048
HEAD_DIM = 64
N_HEADS = 8
HEAD_W = 2 * HEAD_DIM
ATTN_W = N_HEADS * HEAD_W
POOL_W = 1024
POOL_WINDOWS = (2, 4, 8, 16)
POOL_GROUP_W = POOL_W // len(POOL_WINDOWS)
POOL_HALO = 64
IN_COLS = 3 * ATTN_W + POOL_W + 2 * D_MODEL
N_GROUPS = 4
EXPERTS_PER_GROUP = 8
N_EXPERTS = N_GROUPS * EXPERTS_PER_GROUP
D_EXPERT = 512
ROPE_THETA = 10000.0
EPS = 1e-6
LANES = 128
ROUTER_COLS = 128
EXPERT_COL0 = 8
ROW_TILES = D_MODEL // LANES
ROW_PITCH = ROW_TILES + 1
SAFE_SCORE_BOUND = 50.0
SCORE_BOUND_MARGIN = 1.02

COL_Q, COL_K, COL_V, COL_PIN, COL_GA, COL_GP = 0, 1, 2, 3, 4, 6

VMEM_LIMIT = 56 * 1024 * 1024


def _cparams(sem, vmem=VMEM_LIMIT):
    return pltpu.CompilerParams(dimension_semantics=sem, vmem_limit_bytes=vmem)


def _token_slab(ref, n_tok, s):
    return ref[pl.ds(s, n_tok, stride=ROW_PITCH), :]


def _load_token_rows(ref, n_tok):
    return jnp.concatenate([_token_slab(ref, n_tok, s) for s in range(ROW_TILES)], axis=1)


def _store_token_rows(ref, value):
    n_tok = value.shape[0]
    for s in range(ROW_TILES):
        ref[pl.ds(s, n_tok, stride=ROW_PITCH), :] = value[:, s * LANES:(s + 1) * LANES]
    ref[pl.ds(ROW_TILES, n_tok, stride=ROW_PITCH), :] = jnp.zeros((n_tok, LANES), ref.dtype)


def _inproj_kernel(posblk_ref, x0_ref, x1_ref, gattn_ref, w_ref, cos_ref, sin_ref, qkg_ref, bd_ref,
                   o_ref, xn_ref, *, nt0, rc):
    del posblk_ref
    i = pl.program_id(0)
    j = pl.program_id(1)
    tm = xn_ref.shape[0]
    n_chunks = tm // rc

    def normalise(x_ref):
        for c in range(n_chunks):
            xs = x_ref[c * rc:(c + 1) * rc, :]
            ms = jnp.mean(xs * xs, axis=-1, keepdims=True)
            xn_ref[c * rc:(c + 1) * rc, :] = (xs * lax.rsqrt(ms + EPS) * gattn_ref[...]).astype(BF16)

    @pl.when((j == 0) & (i < nt0))
    def _():
        normalise(x0_ref)

    @pl.when((j == 0) & (i >= nt0))
    def _():
        normalise(x1_ref)

    def chunk_acc(c):
        return jnp.dot(xn_ref[c * rc:(c + 1) * rc, :], w_ref[...], preferred_element_type=F32)

    @pl.when(j <= COL_K)
    def _():
        g = jnp.where(j == COL_Q, qkg_ref[0:1, :], qkg_ref[1:2, :])
        lane = lax.broadcasted_iota(I32, (rc, LANES), 1)
        first_half = (lane % HEAD_DIM) < (HEAD_DIM // 2)
        for c in range(n_chunks):
            a = chunk_acc(c)
            cos = cos_ref[c * rc:(c + 1) * rc, :]
            sin = sin_ref[c * rc:(c + 1) * rc, :]
            for t in range(a.shape[1] // LANES):
                at = a[:, t * LANES:(t + 1) * LANES]
                ms = jnp.dot((at * at).astype(BF16), bd_ref[...], preferred_element_type=F32)
                u = at * g[:, t * LANES:(t + 1) * LANES]
                rot = jnp.where(first_half, pltpu.roll(u, LANES - HEAD_DIM // 2, 1),
                                pltpu.roll(u, HEAD_DIM // 2, 1))
                val = (u * cos + rot * sin) * lax.rsqrt(ms + EPS)
                o_ref[c * rc:(c + 1) * rc, t * LANES:(t + 1) * LANES] = val.astype(BF16)

    @pl.when((j == COL_V) | (j == COL_PIN))
    def _():
        for c in range(n_chunks):
            o_ref[c * rc:(c + 1) * rc, :] = chunk_acc(c).astype(BF16)

    @pl.when(j >= COL_GA)
    def _():
        for c in range(n_chunks):
            o_ref[c * rc:(c + 1) * rc, :] = jax.nn.sigmoid(chunk_acc(c)).astype(BF16)


def _in_projection(x0, x1, posblk, gattn, w_in, cos_t, sin_t, qkg, bd, *, tm, tn, rc):
    t0, t1 = x0.shape[0], x1.shape[0]
    nt0, nt1 = t0 // tm, t1 // tm
    nt = nt0 + nt1
    grid_spec = pltpu.PrefetchScalarGridSpec(
        num_scalar_prefetch=1,
        grid=(nt, IN_COLS // tn),
        in_specs=[
            pl.BlockSpec((tm, D_MODEL), lambda i, j, pb: (jnp.minimum(i, nt0 - 1), 0),
                         pipeline_mode=pl.Buffered(1)),
            pl.BlockSpec((tm, D_MODEL), lambda i, j, pb: (jnp.maximum(i - nt0, 0), 0),
                         pipeline_mode=pl.Buffered(1)),
            pl.BlockSpec((1, D_MODEL), lambda i, j, pb: (0, 0)),
            pl.BlockSpec((D_MODEL, tn), lambda i, j, pb: (0, j)),
            pl.BlockSpec((tm, LANES), lambda i, j, pb: (pb[i], 0)),
            pl.BlockSpec((tm, LANES), lambda i, j, pb: (pb[i], 0)),
            pl.BlockSpec((8, tn), lambda i, j, pb: (0, 0)),
            pl.BlockSpec((LANES, LANES), lambda i, j, pb: (0, 0)),
        ],
        out_specs=pl.BlockSpec((tm, tn), lambda i, j, pb: (i, j)),
        scratch_shapes=[pltpu.VMEM((tm, D_MODEL), BF16)],
    )
    return pl.pallas_call(
        functools.partial(_inproj_kernel, nt0=nt0, rc=rc),
        grid_spec=grid_spec,
        out_shape=jax.ShapeDtypeStruct((t0 + t1, IN_COLS), BF16),
        compiler_params=_cparams(("arbitrary", "arbitrary")),
        name="in_projection",
    )(posblk, x0, x1, gattn, w_in, cos_t, sin_t, qkg, bd)


def _attn_kernel(q_ref, k_ref, v_ref, lam_ref, sg_ref, o_ref, vt_ref, acc_ref, m_ref, l_ref, kn_ref,
                 p_ref, *, tk, lam_init):
    i = pl.program_id(2)
    s_len = k_ref.shape[0]
    tq = q_ref.shape[0]
    nk = s_len // tk
    nt_dims = (((1,), (1,)), ((), ()))

    sel_r = lax.broadcasted_iota(I32, (8, HEAD_W), 0)
    sel_l = lax.broadcasted_iota(I32, (8, HEAD_W), 1)
    sel = jnp.where(sel_l // HEAD_DIM == sel_r, 1.0, 0.0).astype(BF16)

    @pl.when(i == 0)
    def _():
        kn2 = jnp.zeros((8, 1), F32)
        for jb in range(nk):
            rows = slice(jb * tk, (jb + 1) * tk)
            vt_ref[:, rows] = v_ref[rows, :].astype(F32).T.astype(BF16)
            kf = k_ref[rows, :].astype(F32)
            n2 = lax.dot_general(sel, (kf * kf).astype(BF16), nt_dims, preferred_element_type=F32)
            kn2 = jnp.maximum(kn2, jnp.max(n2, axis=1, keepdims=True))
        kn_ref[...] = jnp.broadcast_to(kn2, kn_ref.shape)

    q = q_ref[...]
    lane = lax.broadcasted_iota(I32, q.shape, 1)
    zero = jnp.zeros_like(q)
    qc = (jnp.where(lane < HEAD_DIM, q, zero), jnp.where(lane >= HEAD_DIM, q, zero))

    qf = q.astype(F32)
    qn2 = lax.dot_general(sel, (qf * qf).astype(BF16), nt_dims, preferred_element_type=F32)
    bound = jnp.sqrt(qn2 * kn_ref[:, 0:1]) * SCORE_BOUND_MARGIN
    bound_c = (bound[0:1, :], bound[1:2, :])
    use_bound = jnp.max(bound[0:2, :]) <= SAFE_SCORE_BOUND

    l_ref[...] = jnp.zeros(l_ref.shape, F32)
    acc_ref[...] = jnp.zeros(acc_ref.shape, F32)

    def load_blocks(jb):
        off = jb * tk if isinstance(jb, int) else pl.multiple_of(jb * tk, tk)
        return k_ref[pl.ds(off, tk), :], vt_ref[:, pl.ds(off, tk)]

    def bounded_weights(jb, slot):
        kb, _ = load_blocks(jb)
        for c in range(2):
            st = lax.dot_general(kb, qc[c], nt_dims, preferred_element_type=F32)
            p = jnp.exp2(st - bound_c[c])
            l_ref[c] += jnp.sum(p, axis=0, keepdims=True)
            p_ref[slot, c] = p.astype(BF16)

    def weighted_values(jb, slot):
        _, vtb = load_blocks(jb)
        for c in range(2):
            acc_ref[c] += jnp.dot(vtb, p_ref[slot, c], preferred_element_type=F32)

    def bounded_pair(m, carry):
        j0 = 2 * m
        bounded_weights(j0 + 1, 1)
        weighted_values(j0, 0)
        bounded_weights(j0 + 2, 0)
        weighted_values(j0 + 1, 1)
        return carry

    def running_max_step(jb, carry):
        kb, vtb = load_blocks(jb)
        for c in range(2):
            st = lax.dot_general(kb, qc[c], nt_dims, preferred_element_type=F32)
            m_old = m_ref[c]
            m_new = jnp.maximum(m_old, jnp.max(st, axis=0, keepdims=True))
            alpha = jnp.exp2(m_old - m_new)
            p = jnp.exp2(st - m_new)
            l_ref[c] = alpha * l_ref[c] + jnp.sum(p, axis=0, keepdims=True)
            acc_ref[c] = alpha * acc_ref[c] + jnp.dot(vtb, p.astype(BF16), preferred_element_type=F32)
            m_ref[c] = m_new
        return carry

    @pl.when(use_bound)
    def _():
        bounded_weights(0, 0)
        lax.fori_loop(0, nk // 2 - 1, bounded_pair, 0)
        bounded_weights(nk - 1, 1)
        weighted_values(nk - 2, 0)
        weighted_values(nk - 1, 1)

    @pl.when(jnp.logical_not(use_bound))
    def _():
        m_ref[...] = jnp.full(m_ref.shape, -jnp.inf, F32)
        lax.fori_loop(0, nk, running_max_step, 0)

    lq1, lk1, lq2, lk2 = lam_ref[0:1, :], lam_ref[1:2, :], lam_ref[2:3, :], lam_ref[3:4, :]
    lam = (jnp.exp(jnp.sum(lq1 * lk1, axis=-1, keepdims=True))
           - jnp.exp(jnp.sum(lq2 * lk2, axis=-1, keepdims=True)) + lam_init)
    ot = acc_ref[0] / l_ref[0] - lam * (acc_ref[1] / l_ref[1])
    ms = jnp.mean(ot * ot, axis=0, keepdims=True)
    on = (ot * lax.rsqrt(ms + EPS)).T
    o_ref[...] = (on * sg_ref[...] * (1.0 - lam_init)).astype(BF16)


def _attention(proj, tok_off, batch, s_len, lam_pack, subln_g, *, tq, tk, lam_init):
    qb0 = tok_off // tq
    sb0 = tok_off // s_len
    nq = s_len // tq
    return pl.pallas_call(
        functools.partial(_attn_kernel, tk=tk, lam_init=lam_init),
        grid=(batch, N_HEADS, nq),
        in_specs=[
            pl.BlockSpec((tq, HEAD_W), lambda b, h, i: (qb0 + b * nq + i, COL_Q * N_HEADS + h)),
            pl.BlockSpec((s_len, HEAD_W), lambda b, h, i: (sb0 + b, COL_K * N_HEADS + h)),
            pl.BlockSpec((s_len, HEAD_W), lambda b, h, i: (sb0 + b, COL_V * N_HEADS + h)),
            pl.BlockSpec((8, HEAD_DIM), lambda b, h, i: (0, 0)),
            pl.BlockSpec((1, HEAD_W), lambda b, h, i: (0, 0)),
        ],
        out_specs=pl.BlockSpec((tq, HEAD_W), lambda b, h, i: (b * nq + i, h)),
        out_shape=jax.ShapeDtypeStruct((batch * s_len, ATTN_W), BF16),
        scratch_shapes=[
            pltpu.VMEM((HEAD_W, s_len), BF16),
            pltpu.VMEM((2, HEAD_W, tq), F32),
            pltpu.VMEM((2, 1, tq), F32),
            pltpu.VMEM((2, 1, tq), F32),
            pltpu.VMEM((8, LANES), F32),
            pltpu.VMEM((2, 2, tk, tq), BF16),
        ],
        compiler_params=_cparams(("arbitrary", "arbitrary", "arbitrary")),
        name="diff_attention",
    )(proj, proj, proj, lam_pack, subln_g)


def _post_kernel(pos_ref, len_ref, x0_ref, x1_ref, o0_ref, o1_ref, pin_ref, prev_ref, next_ref,
                 ga_ref, gp_ref, wau_ref, wgrp_ref, pscale_ref, wpu_ref, wout_ref, gffn_ref,
                 wrc_ref, br_ref, xnew_ref, ids_ref, wts_ref, *, nt0):
    i = pl.program_id(0)
    tm = pin_ref.shape[0]
    p0 = pos_ref[i]
    s_len = len_ref[i]
    first = i < nt0

    o = jnp.where(first, o0_ref[...], o1_ref[...])
    attn_d = jnp.dot(o, wau_ref[...], preferred_element_type=F32)

    ext = jnp.concatenate([prev_ref[...], pin_ref[...], next_ref[...]], axis=0)
    ke = tm + 2 * POOL_HALO
    r_band = lax.broadcasted_iota(I32, (tm, ke), 0) + p0
    c_band = lax.broadcasted_iota(I32, (tm, ke), 1) + (p0 - POOL_HALO)
    r_cnt = lax.broadcasted_iota(I32, (tm, POOL_GROUP_W), 0) + p0
    pooled = []
    for gi, w in enumerate(POOL_WINDOWS):
        back, fwd = w // 2, w - 1 - w // 2
        lo = jnp.maximum(r_band - back, 0)
        hi = jnp.minimum(r_band + fwd, s_len - 1)
        band = jnp.where((c_band >= lo) & (c_band <= hi), 1.0, 0.0).astype(BF16)
        cnt = jnp.minimum(r_cnt + fwd, s_len - 1) - jnp.maximum(r_cnt - back, 0) + 1
        cols = slice(gi * POOL_GROUP_W, (gi + 1) * POOL_GROUP_W)
        wsum = jnp.dot(band, ext[:, cols], preferred_element_type=F32)
        y = wsum / cnt.astype(F32) - pin_ref[:, cols].astype(F32)
        pooled.append(jnp.dot(y.astype(BF16), wgrp_ref[gi], preferred_element_type=F32))
    pooled = (jnp.concatenate(pooled, axis=1) * pscale_ref[...]).astype(BF16)
    pool_d = jnp.dot(pooled, wpu_ref[...], preferred_element_type=F32)

    merged = ga_ref[...].astype(F32) * attn_d + gp_ref[...].astype(F32) * pool_d
    x = jnp.where(first, x0_ref[...], x1_ref[...])
    xnew = x + jnp.dot(merged.astype(BF16), wout_ref[...], preferred_element_type=F32)
    _store_token_rows(xnew_ref, xnew)

    ms = jnp.mean(xnew * xnew, axis=-1, keepdims=True)
    hn = xnew * lax.rsqrt(ms + EPS) * gffn_ref[...]
    hi_part = hn.astype(BF16)
    lo_part = (hn - hi_part.astype(F32)).astype(BF16)
    hw = jnp.dot(hi_part, wrc_ref[...], preferred_element_type=F32)
    logits = (hw[:, 0:ROUTER_COLS]
              + (hw[:, ROUTER_COLS:]
                 + jnp.dot(lo_part, wrc_ref[:, 0:ROUTER_COLS], preferred_element_type=F32))
              + br_ref[...])
    lt = logits.T

    gl = lt[0:N_GROUPS, :]
    g_iota = lax.broadcasted_iota(I32, gl.shape, 0)
    g_max = jnp.max(gl, axis=0, keepdims=True)
    g_sel = jnp.min(jnp.where(gl == g_max, g_iota, N_GROUPS), axis=0, keepdims=True)
    g_w = 1.0 / jnp.sum(jnp.exp(gl - g_max), axis=0, keepdims=True)
    el = lt[EXPERT_COL0:EXPERT_COL0 + N_EXPERTS, :]
    e_iota = lax.broadcasted_iota(I32, el.shape, 0)
    neg = jnp.float32(-jnp.inf)
    el = jnp.where(e_iota // EXPERTS_PER_GROUP == g_sel, el, neg)
    v1 = jnp.max(el, axis=0, keepdims=True)
    i1 = jnp.min(jnp.where(el == v1, e_iota, N_EXPERTS), axis=0, keepdims=True)
    el2 = jnp.where(e_iota == i1, neg, el)
    v2 = jnp.max(el2, axis=0, keepdims=True)
    i2 = jnp.min(jnp.where(el2 == v2, e_iota, N_EXPERTS), axis=0, keepdims=True)
    d = jnp.exp(v2 - v1)
    w1 = g_w / (1.0 + d)
    w2 = g_w * d / (1.0 + d)
    row = lax.broadcasted_iota(I32, (8, tm), 0)
    ids_ref[...] = jnp.where(row == 0, i1, jnp.where(row == 1, i2, 0))
    wts_ref[...] = jnp.where(row == 0, w1, jnp.where(row == 1, w2, 0.0))


def _post_block(x0, x1, o0, o1, proj, tile_pos, tile_len, wau, wgrp, pscale, wpu, wout, gffn,
                wrc, br, *, tm):
    t0, t1 = x0.shape[0], x1.shape[0]
    t = t0 + t1
    nt0 = t0 // tm
    nt = t // tm
    hb = tm // POOL_HALO
    n_hblk = t // POOL_HALO
    const = lambda *shape: pl.BlockSpec(shape, lambda i, tp, tl: (0,) * len(shape),
                                        pipeline_mode=pl.Buffered(1))
    grid_spec = pltpu.PrefetchScalarGridSpec(
        num_scalar_prefetch=2,
        grid=(nt,),
        in_specs=[
            pl.BlockSpec((tm, D_MODEL), lambda i, tp, tl: (jnp.minimum(i, nt0 - 1), 0)),
            pl.BlockSpec((tm, D_MODEL), lambda i, tp, tl: (jnp.maximum(i - nt0, 0), 0)),
            pl.BlockSpec((tm, ATTN_W), lambda i, tp, tl: (jnp.minimum(i, nt0 - 1), 0)),
            pl.BlockSpec((tm, ATTN_W), lambda i, tp, tl: (jnp.maximum(i - nt0, 0), 0)),
            pl.BlockSpec((tm, POOL_W), lambda i, tp, tl: (i, COL_PIN)),
            pl.BlockSpec((POOL_HALO, POOL_W), lambda i, tp, tl: (jnp.maximum(i * hb - 1, 0), COL_PIN)),
            pl.BlockSpec((POOL_HALO, POOL_W), lambda i, tp, tl: (jnp.minimum((i + 1) * hb, n_hblk - 1), COL_PIN)),
            pl.BlockSpec((tm, D_MODEL), lambda i, tp, tl: (i, COL_GA // 2)),
            pl.BlockSpec((tm, D_MODEL), lambda i, tp, tl: (i, COL_GP // 2)),
            const(ATTN_W, D_MODEL),
            const(len(POOL_WINDOWS), POOL_GROUP_W, POOL_GROUP_W),
            const(1, POOL_W),
            const(POOL_W, D_MODEL),
            const(D_MODEL, D_MODEL),
            const(1, D_MODEL),
            const(D_MODEL, 2 * ROUTER_COLS),
            const(1, ROUTER_COLS),
        ],
        out_specs=[
            pl.BlockSpec((tm * ROW_PITCH, LANES), lambda i, tp, tl: (i, 0)),
            pl.BlockSpec((8, tm), lambda i, tp, tl: (0, i)),
            pl.BlockSpec((8, tm), lambda i, tp, tl: (0, i)),
        ],
    )
    return pl.pallas_call(
        functools.partial(_post_kernel, nt0=nt0),
        grid_spec=grid_spec,
        out_shape=[
            jax.ShapeDtypeStruct((t * ROW_PITCH, LANES), F32),
            jax.ShapeDtypeStruct((8, t), I32),
            jax.ShapeDtypeStruct((8, t), F32),
        ],
        compiler_params=_cparams(("arbitrary",)),
        name="post_block",
    )(tile_pos, tile_len, x0, x1, o0, o1, proj, proj, proj, proj, proj, wau, wgrp, pscale, wpu,
      wout, gffn, wrc, br)


GATHER_UNROLL = 8


def _row_gather_start(src_hbm, idx_ref, base, n_rows, buf, sem):
    def body(r, carry):
        src0 = idx_ref[base + r] * ROW_PITCH
        dst0 = r * ROW_PITCH
        pltpu.make_async_copy(src_hbm.at[pl.ds(src0, ROW_TILES)], buf.at[pl.ds(dst0, ROW_TILES)], sem).start()
        return carry
    lax.fori_loop(0, n_rows, body, 0, unroll=GATHER_UNROLL)


def _row_gather_wait(src_hbm, n_rows, buf, sem):
    n = n_rows * ROW_TILES
    pltpu.make_async_copy(src_hbm.at[pl.ds(0, n)], buf.at[pl.ds(0, n)], sem).wait()


def _moe_kernel(te_ref, tf_ref, tn_ref, ts_ref, nu_ref, src_ref, x_hbm, gffn_ref, wg_hbm, wu_hbm, wd_hbm,
                y_ref, xbuf, sem, wgf, wuf, wdf, wsem, wgb, wub, wdb):
    i = pl.program_id(0)
    tm = y_ref.shape[0] // ROW_PITCH
    n_used = nu_ref[0]
    slot = lax.rem(i, 2)

    def weight_copies(e, s):
        return (pltpu.make_async_copy(wg_hbm.at[e], wgf.at[s], wsem.at[s]),
                pltpu.make_async_copy(wu_hbm.at[e], wuf.at[s], wsem.at[s]),
                pltpu.make_async_copy(wd_hbm.at[e], wdf.at[s], wsem.at[s]))

    @pl.when(i == 0)
    def _():
        _row_gather_start(x_hbm, src_ref, 0, tm, xbuf.at[0], sem.at[0])
        for cp in weight_copies(te_ref[0], 0):
            cp.start()

    @pl.when(i + 1 < n_used)
    def _():
        _row_gather_start(x_hbm, src_ref, (i + 1) * tm, tm, xbuf.at[1 - slot], sem.at[1 - slot])

    @pl.when(i < n_used)
    def _():
        @pl.when(tf_ref[i] == 1)
        def _():
            ws = ts_ref[i]
            for cp in weight_copies(te_ref[i], ws):
                cp.wait()
            wgb[...] = wgf[ws].astype(BF16)
            wub[...] = wuf[ws].astype(BF16)
            wdb[...] = wdf[ws].astype(BF16)

            @pl.when(tn_ref[i] >= 0)
            def _():
                for cp in weight_copies(tn_ref[i], 1 - ws):
                    cp.start()

        _row_gather_wait(x_hbm, tm, xbuf.at[slot], sem.at[slot])
        xs = _load_token_rows(xbuf.at[slot], tm)
        ms = jnp.mean(xs * xs, axis=-1, keepdims=True)
        hn = (xs * lax.rsqrt(ms + EPS) * gffn_ref[...]).astype(BF16)
        gate = jnp.dot(hn, wgb[...], preferred_element_type=F32)
        up = jnp.dot(hn, wub[...], preferred_element_type=F32)
        h = jax.nn.silu(gate) * up
        _store_token_rows(y_ref, jnp.dot(h.astype(BF16), wdb[...], preferred_element_type=F32))

    @pl.when(i >= n_used)
    def _():
        y_ref[...] = jnp.zeros(y_ref.shape, y_ref.dtype)


def _moe_ffn(xnew, tile_e, tile_first, tile_next, tile_wslot, n_used, src_row, gffn, w_gate, w_up, w_down,
             *, tm):
    p_rows = src_row.shape[0]
    n_tiles = p_rows // tm
    grid_spec = pltpu.PrefetchScalarGridSpec(
        num_scalar_prefetch=6,
        grid=(n_tiles,),
        in_specs=[
            pl.BlockSpec(memory_space=pl.ANY),
            pl.BlockSpec((1, D_MODEL), lambda i, *_: (0, 0)),
            pl.BlockSpec(memory_space=pl.ANY),
            pl.BlockSpec(memory_space=pl.ANY),
            pl.BlockSpec(memory_space=pl.ANY),
        ],
        out_specs=pl.BlockSpec((tm * ROW_PITCH, LANES), lambda i, *_: (i, 0)),
        scratch_shapes=[
            pltpu.VMEM((2, tm * ROW_PITCH, LANES), F32),
            pltpu.SemaphoreType.DMA((2,)),
            pltpu.VMEM((2, D_MODEL, D_EXPERT), F32),
            pltpu.VMEM((2, D_MODEL, D_EXPERT), F32),
            pltpu.VMEM((2, D_EXPERT, D_MODEL), F32),
            pltpu.SemaphoreType.DMA((2,)),
            pltpu.VMEM((D_MODEL, D_EXPERT), BF16),
            pltpu.VMEM((D_MODEL, D_EXPERT), BF16),
            pltpu.VMEM((D_EXPERT, D_MODEL), BF16),
        ],
    )
    return pl.pallas_call(
        _moe_kernel,
        grid_spec=grid_spec,
        out_shape=jax.ShapeDtypeStruct((p_rows * ROW_PITCH, LANES), F32),
        compiler_params=_cparams(("arbitrary",)),
        name="moe_ffn",
    )(tile_e, tile_first, tile_next, tile_wslot, n_used, src_row, xnew, gffn, w_gate, w_up, w_down)


def _combine_kernel(pos_ref, xnew_ref, wts_ref, y_hbm, out0_ref, out1_ref, ybuf, sem, *, nt0, n_tok):
    i = pl.program_id(0)
    nt = pl.num_programs(0)
    tm = xnew_ref.shape[0] // ROW_PITCH
    slot = lax.rem(i, 2)

    def start(tile, s):
        _row_gather_start(y_hbm, pos_ref, tile * tm, tm, ybuf.at[s, 0], sem.at[s])
        _row_gather_start(y_hbm, pos_ref, n_tok + tile * tm, tm, ybuf.at[s, 1], sem.at[s])

    @pl.when(i == 0)
    def _():
        start(0, 0)

    @pl.when(i + 1 < nt)
    def _():
        start(i + 1, 1 - slot)

    _row_gather_wait(y_hbm, tm, ybuf.at[slot, 0], sem.at[slot])
    _row_gather_wait(y_hbm, tm, ybuf.at[slot, 1], sem.at[slot])
    wt = wts_ref[...].T
    w1 = jnp.broadcast_to(wt[:, 0:1], (tm, LANES))
    w2 = jnp.broadcast_to(wt[:, 1:2], (tm, LANES))

    def write(out_ref):
        for s in range(ROW_TILES):
            out_ref[:, s * LANES:(s + 1) * LANES] = (
                _token_slab(xnew_ref, tm, s)
                + (w1 * _token_slab(ybuf.at[slot, 0], tm, s) + w2 * _token_slab(ybuf.at[slot, 1], tm, s)))

    @pl.when(i < nt0)
    def _():
        write(out0_ref)

    @pl.when(i >= nt0)
    def _():
        write(out1_ref)


def _combine(xnew, y_sorted, wts, pos, t0, *, tm):
    t = xnew.shape[0] // ROW_PITCH
    nt0 = t0 // tm
    nt = t // tm
    grid_spec = pltpu.PrefetchScalarGridSpec(
        num_scalar_prefetch=1,
        grid=(nt,),
        in_specs=[
            pl.BlockSpec((tm * ROW_PITCH, LANES), lambda i, ps: (i, 0)),
            pl.BlockSpec((8, tm), lambda i, ps: (0, i)),
            pl.BlockSpec(memory_space=pl.ANY),
        ],
        out_specs=[
            pl.BlockSpec((tm, D_MODEL), lambda i, ps: (jnp.minimum(i, nt0 - 1), 0)),
            pl.BlockSpec((tm, D_MODEL), lambda i, ps: (jnp.maximum(i - nt0, 0), 0)),
        ],
        scratch_shapes=[
            pltpu.VMEM((2, 2, tm * ROW_PITCH, LANES), F32),
            pltpu.SemaphoreType.DMA((2,)),
        ],
    )
    return pl.pallas_call(
        functools.partial(_combine_kernel, nt0=nt0, n_tok=t),
        grid_spec=grid_spec,
        out_shape=[
            jax.ShapeDtypeStruct((t0, D_MODEL), F32),
            jax.ShapeDtypeStruct((t - t0, D_MODEL), F32),
        ],
        compiler_params=_cparams(("arbitrary",)),
        name="moe_combine",
    )(pos, xnew, wts, y_sorted)


def _rope_tables(max_len):
    inv = 1.0 / (ROPE_THETA ** (jnp.arange(0, HEAD_DIM, 2, dtype=F32) / HEAD_DIM))
    ang = jnp.arange(max_len, dtype=F32)[:, None] * inv[None, :]
    ang = jnp.concatenate([ang, ang, ang, ang], axis=-1)
    lane = jnp.arange(LANES)
    sign = jnp.where((lane % HEAD_DIM) < HEAD_DIM // 2, -1.0, 1.0).astype(F32)
    return jnp.cos(ang), jnp.sin(ang) * sign[None, :]


def _tile_tables(groups, tm):
    pos, length = [], []
    for batch, s_len in groups:
        for _ in range(batch):
            for p in range(0, s_len, tm):
                pos.append(p)
                length.append(s_len)
    return np.asarray(pos, np.int32), np.asarray(length, np.int32)


def _routing_tables(ids, n_tok, tm):
    e_flat = ids[0:2, :].reshape(-1)
    onehot = (e_flat[:, None] == jnp.arange(N_EXPERTS, dtype=I32)[None, :]).astype(I32)
    csum = jnp.cumsum(onehot, axis=0)
    rank = jnp.sum(csum * onehot, axis=1) - 1
    counts = csum[-1]
    padded = ((counts + tm - 1) // tm) * tm
    ends = jnp.cumsum(padded)
    starts = ends - padded
    dst = (starts[e_flat] + rank).astype(I32)
    p_rows = 2 * n_tok + N_EXPERTS * tm
    n_tiles = p_rows // tm
    tok = jnp.tile(jnp.arange(n_tok, dtype=I32), 2)
    src_row = jnp.zeros((p_rows,), I32).at[dst].set(tok)
    n_used = (ends[-1] // tm).astype(I32)
    tile_start = jnp.arange(n_tiles, dtype=I32) * tm
    tile_e = jnp.sum((ends[None, :] <= tile_start[:, None]).astype(I32), axis=1)
    tile_e = jnp.minimum(tile_e, N_EXPERTS - 1)
    active = jnp.arange(n_tiles, dtype=I32) < n_used
    last_e = tile_e[jnp.maximum(n_used - 1, 0)]
    tile_e = jnp.where(active, tile_e, last_e)
    tile_first = (active & (tile_start == starts[tile_e])).astype(I32)
    tile_wslot = lax.rem(jnp.cumsum(tile_first) - 1, 2).astype(I32)
    e_ids = jnp.arange(N_EXPERTS, dtype=I32)
    later_used = (counts[None, :] > 0) & (e_ids[None, :] > e_ids[:, None])
    next_used = jnp.min(jnp.where(later_used, e_ids[None, :], N_EXPERTS), axis=1)
    next_used = jnp.where(next_used == N_EXPERTS, -1, next_used).astype(I32)
    tile_next = next_used[tile_e]
    return tile_e, tile_first, tile_next, tile_wslot, n_used.reshape(1), src_row, dst


def _forward(x0, x1, groups, params, *, tm_proj, tn_proj, rc_proj, tq, tk, tm_post, tm_moe, tm_comb):
    (attn_norm_g, w_in, q_norm_g, k_norm_g, lambda_q1, lambda_k1, lambda_q2, lambda_k2, subln_g,
     w_attn_up, w_pool_grp, pool_scale, w_pool_up, w_out, ffn_norm_g, w_group_router,
     b_group_router, w_expert_router, b_expert_router, w_gate, w_up, w_down) = params
    t0, t1 = x0.shape[0], x1.shape[0]
    n_tok = t0 + t1
    lam_init = 0.8 - 0.6 * math.exp(-0.3 * 0)
    max_len = max(s for _, s in groups)

    cos_t, sin_t = _rope_tables(max_len)
    pos_proj, _ = _tile_tables(groups, tm_proj)
    q_scale = HEAD_DIM ** -0.5 * math.log2(math.e)
    qkg = jnp.zeros((8, ATTN_W), F32)
    qkg = qkg.at[0].set(jnp.tile(q_norm_g, ATTN_W // HEAD_DIM) * q_scale)
    qkg = qkg.at[1].set(jnp.tile(k_norm_g, ATTN_W // HEAD_DIM))
    seg = np.arange(LANES) // HEAD_DIM
    bd = jnp.asarray((seg[:, None] == seg[None, :]).astype(np.float32) / HEAD_DIM, BF16)

    proj = _in_projection(x0, x1, jnp.asarray(pos_proj // tm_proj), attn_norm_g[None, :],
                          w_in.astype(BF16), cos_t, sin_t, qkg, bd, tm=tm_proj, tn=tn_proj, rc=rc_proj)

    lam_pack = jnp.zeros((8, HEAD_DIM), F32)
    lam_pack = lam_pack.at[0].set(lambda_q1).at[1].set(lambda_k1).at[2].set(lambda_q2).at[3].set(lambda_k2)
    outs = []
    tok_off = 0
    for batch, s_len in groups:
        outs.append(_attention(proj, tok_off, batch, s_len, lam_pack, subln_g[None, :],
                               tq=min(tq, s_len), tk=min(tk, s_len), lam_init=lam_init))
        tok_off += batch * s_len

    tile_pos, tile_len = _tile_tables(groups, tm_post)
    w_router = jnp.zeros((D_MODEL, ROUTER_COLS), F32)
    w_router = w_router.at[:, 0:N_GROUPS].set(w_group_router)
    w_router = w_router.at[:, EXPERT_COL0:EXPERT_COL0 + N_EXPERTS].set(w_expert_router)
    b_router = jnp.zeros((1, ROUTER_COLS), F32)
    b_router = b_router.at[0, 0:N_GROUPS].set(b_group_router)
    b_router = b_router.at[0, EXPERT_COL0:EXPERT_COL0 + N_EXPERTS].set(b_expert_router)
    wr_hi = w_router.astype(BF16)
    wr_lo = (w_router - wr_hi.astype(F32)).astype(BF16)
    xnew, ids, wts = _post_block(
        x0, x1, outs[0], outs[1], proj, jnp.asarray(tile_pos), jnp.asarray(tile_len),
        w_attn_up.astype(BF16), w_pool_grp.astype(BF16), pool_scale[None, :], w_pool_up.astype(BF16),
        w_out.astype(BF16), ffn_norm_g[None, :], jnp.concatenate([wr_hi, wr_lo], axis=1), b_router,
        tm=tm_post)

    tile_e, tile_first, tile_next, tile_wslot, n_used, src_row, dst = _routing_tables(ids, n_tok, tm_moe)
    y_sorted = _moe_ffn(
        xnew, tile_e, tile_first, tile_next, tile_wslot, n_used, src_row, ffn_norm_g[None, :],
        w_gate.reshape(N_EXPERTS, D_MODEL, D_EXPERT), w_up.reshape(N_EXPERTS, D_MODEL, D_EXPERT),
        w_down.reshape(N_EXPERTS, D_EXPERT, D_MODEL), tm=tm_moe)
    return _combine(xnew, y_sorted, wts, dst, t0, tm=tm_comb)


def kernel(x_prompt, x_sample, attn_norm_g, w_in, q_norm_g, k_norm_g, lambda_q1, lambda_k1, lambda_q2,
           lambda_k2, subln_g, w_attn_up, w_pool_grp, pool_scale, w_pool_up, w_out, ffn_norm_g,
           w_group_router, b_group_router, w_expert_router, b_expert_router, w_gate, w_up, w_down):
    assert attn_norm_g.shape[0] == 1, "single-layer stack"
    params = tuple(p[0] for p in (
        attn_norm_g, w_in, q_norm_g, k_norm_g, lambda_q1, lambda_k1, lambda_q2, lambda_k2, subln_g,
        w_attn_up, w_pool_grp, pool_scale, w_pool_up, w_out, ffn_norm_g, w_group_router,
        b_group_router, w_expert_router, b_expert_router, w_gate, w_up, w_down))
    groups = (x_prompt.shape[:2], x_sample.shape[:2])
    y0, y1 = _forward(
        x_prompt.reshape(-1, D_MODEL), x_sample.reshape(-1, D_MODEL), groups, params,
        tm_proj=1024, tn_proj=1024, rc_proj=512, tq=1024, tk=512, tm_post=256, tm_moe=256, tm_comb=256)
    return y0.reshape(x_prompt.shape), y1.reshape(x_sample.shape)
```

```python
import functools
import math

import numpy as np
import jax
import jax.numpy as jnp
from jax import lax
from jax.experimental import pallas as pl
from jax.experimental.pallas import tpu as pltpu

F32 = jnp.float32
BF16 = jnp.bfloat16
I32 = jnp.int32

D_MODEL = 2048
HEAD_DIM = 64
N_HEADS = 8
HEAD_W = 2 * HEAD_DIM
ATTN_W = N_HEADS * HEAD_W
POOL_W = 1024
POOL_WINDOWS = (2, 4, 8, 16)
POOL_GROUP_W = POOL_W // len(POOL_WINDOWS)
POOL_HALO = 64
IN_COLS = 3 * ATTN_W + POOL_W + 2 * D_MODEL
N_GROUPS = 4
EXPERTS_PER_GROUP = 8
N_EXPERTS = N_GROUPS * EXPERTS_PER_GROUP
D_EXPERT = 512
ROPE_THETA = 10000.0
EPS = 1e-6
LANES = 128
ROUTER_COLS = 128
EXPERT_COL0 = 8
ROW_TILES = D_MODEL // LANES
ROW_PITCH = ROW_TILES + 1
SAFE_SCORE_BOUND = 50.0
SCORE_BOUND_MARGIN = 1.02

COL_Q, COL_K, COL_V, COL_PIN, COL_GA, COL_GP = 0, 1, 2, 3, 4, 6

VMEM_LIMIT = 56 * 1024 * 1024


def _cparams(sem, vmem=VMEM_LIMIT):
    return pltpu.CompilerParams(dimension_semantics=sem, vmem_limit_bytes=vmem)


def _token_slab(ref, n_tok, s):
    return ref[pl.ds(s, n_tok, stride=ROW_PITCH), :]


def _load_token_rows(ref, n_tok):
    return jnp.concatenate([_token_slab(ref, n_tok, s) for s in range(ROW_TILES)], axis=1)


def _store_token_rows(ref, value):
    n_tok = value.shape[0]
    for s in range(ROW_TILES):
        ref[pl.ds(s, n_tok, stride=ROW_PITCH), :] = value[:, s * LANES:(s + 1) * LANES]
    ref[pl.ds(ROW_TILES, n_tok, stride=ROW_PITCH), :] = jnp.zeros((n_tok, LANES), ref.dtype)


def _inproj_kernel(posblk_ref, x0_hbm, x1_hbm, gattn_ref, w_ref, cos_ref, sin_ref, qkg_ref, bd_ref,
                   o_ref, xn_ref, xbuf, xsem, *, nt0, rc):
    del posblk_ref
    i = pl.program_id(0)
    j = pl.program_id(1)
    nt = pl.num_programs(0)
    tm = xn_ref.shape[0]
    n_chunks = tm // rc

    def x_tile_copy(x_hbm, tile):
        return pltpu.make_async_copy(x_hbm.at[pl.ds(tile * tm, tm)], xbuf, xsem)

    def start_x_tile(tile):
        @pl.when(tile < nt0)
        def _():
            x_tile_copy(x0_hbm, tile).start()

        @pl.when(tile >= nt0)
        def _():
            x_tile_copy(x1_hbm, tile - nt0).start()

    @pl.when(j == 0)
    def _():
        @pl.when(i == 0)
        def _():
            start_x_tile(i)

        x_tile_copy(x0_hbm, 0).wait()
        for c in range(n_chunks):
            xs = xbuf[c * rc:(c + 1) * rc, :]
            ms = jnp.mean(xs * xs, axis=-1, keepdims=True)
            xn_ref[c * rc:(c + 1) * rc, :] = (xs * lax.rsqrt(ms + EPS) * gattn_ref[...]).astype(BF16)

        @pl.when(i + 1 < nt)
        def _():
            start_x_tile(i + 1)

    def chunk_acc(c):
        return jnp.dot(xn_ref[c * rc:(c + 1) * rc, :], w_ref[...], preferred_element_type=F32)

    @pl.when(j <= COL_K)
    def _():
        g = jnp.where(j == COL_Q, qkg_ref[0:1, :], qkg_ref[1:2, :])
        lane = lax.broadcasted_iota(I32, (rc, LANES), 1)
        first_half = (lane % HEAD_DIM) < (HEAD_DIM // 2)
        for c in range(n_chunks):
            a = chunk_acc(c)
            cos = cos_ref[c * rc:(c + 1) * rc, :]
            sin = sin_ref[c * rc:(c + 1) * rc, :]
            for t in range(a.shape[1] // LANES):
                at = a[:, t * LANES:(t + 1) * LANES]
                ms = jnp.dot((at * at).astype(BF16), bd_ref[...], preferred_element_type=F32)
                u = at * g[:, t * LANES:(t + 1) * LANES]
                rot = jnp.where(first_half, pltpu.roll(u, LANES - HEAD_DIM // 2, 1),
                                pltpu.roll(u, HEAD_DIM // 2, 1))
                val = (u * cos + rot * sin) * lax.rsqrt(ms + EPS)
                o_ref[c * rc:(c + 1) * rc, t * LANES:(t + 1) * LANES] = val.astype(BF16)

    @pl.when((j == COL_V) | (j == COL_PIN))
    def _():
        for c in range(n_chunks):
            o_ref[c * rc:(c + 1) * rc, :] = chunk_acc(c).astype(BF16)

    @pl.when(j >= COL_GA)
    def _():
        for c in range(n_chunks):
            o_ref[c * rc:(c + 1) * rc, :] = jax.nn.sigmoid(chunk_acc(c)).astype(BF16)


def _in_projection(x0, x1, posblk, gattn, w_in, cos_t, sin_t, qkg, bd, *, tm, tn, rc):
    t0, t1 = x0.shape[0], x1.shape[0]
    nt0, nt1 = t0 // tm, t1 // tm
    nt = nt0 + nt1
    grid_spec = pltpu.PrefetchScalarGridSpec(
        num_scalar_prefetch=1,
        grid=(nt, IN_COLS // tn),
        in_specs=[
            pl.BlockSpec(memory_space=pl.ANY),
            pl.BlockSpec(memory_space=pl.ANY),
            pl.BlockSpec((1, D_MODEL), lambda i, j, pb: (0, 0)),
            pl.BlockSpec((D_MODEL, tn), lambda i, j, pb: (0, j)),
            pl.BlockSpec((tm, LANES), lambda i, j, pb: (pb[i], 0)),
            pl.BlockSpec((tm, LANES), lambda i, j, pb: (pb[i], 0)),
            pl.BlockSpec((8, tn), lambda i, j, pb: (0, 0)),
            pl.BlockSpec((LANES, LANES), lambda i, j, pb: (0, 0)),
        ],
        out_specs=pl.BlockSpec((tm, tn), lambda i, j, pb: (i, j)),
        scratch_shapes=[
            pltpu.VMEM((tm, D_MODEL), BF16),
            pltpu.VMEM((tm, D_MODEL), F32),
            pltpu.SemaphoreType.DMA(()),
        ],
    )
    return pl.pallas_call(
        functools.partial(_inproj_kernel, nt0=nt0, rc=rc),
        grid_spec=grid_spec,
        out_shape=jax.ShapeDtypeStruct((t0 + t1, IN_COLS), BF16),
        compiler_params=_cparams(("arbitrary", "arbitrary")),
        name="in_projection",
    )(posblk, x0, x1, gattn, w_in, cos_t, sin_t, qkg, bd)


def _attn_kernel(q_ref, k_ref, v_ref, lam_ref, sg_ref, o_ref, vt_ref, acc_ref, m_ref, l_ref, kn_ref,
                 p_ref, *, tk, lam_init):
    i = pl.program_id(2)
    s_len = k_ref.shape[0]
    tq = q_ref.shape[0]
    nk = s_len // tk
    nt_dims = (((1,), (1,)), ((), ()))

    sel_r = lax.broadcasted_iota(I32, (8, HEAD_W), 0)
    sel_l = lax.broadcasted_iota(I32, (8, HEAD_W), 1)
    sel = jnp.where(sel_l // HEAD_DIM == sel_r, 1.0, 0.0).astype(BF16)

    @pl.when(i == 0)
    def _():
        kn2 = jnp.zeros((8, 1), F32)
        for jb in range(nk):
            rows = slice(jb * tk, (jb + 1) * tk)
            vt_ref[:, rows] = v_ref[rows, :].astype(F32).T.astype(BF16)
            kf = k_ref[rows, :].astype(F32)
            n2 = lax.dot_general(sel, (kf * kf).astype(BF16), nt_dims, preferred_element_type=F32)
            kn2 = jnp.maximum(kn2, jnp.max(n2, axis=1, keepdims=True))
        kn_ref[...] = jnp.broadcast_to(kn2, kn_ref.shape)

    q = q_ref[...]
    lane = lax.broadcasted_iota(I32, q.shape, 1)
    zero = jnp.zeros_like(q)
    qc = (jnp.where(lane < HEAD_DIM, q, zero), jnp.where(lane >= HEAD_DIM, q, zero))

    qf = q.astype(F32)
    qn2 = lax.dot_general(sel, (qf * qf).astype(BF16), nt_dims, preferred_element_type=F32)
    bound = jnp.sqrt(qn2 * kn_ref[:, 0:1]) * SCORE_BOUND_MARGIN
    bound_c = (bound[0:1, :], bound[1:2, :])
    use_bound = jnp.max(bound[0:2, :]) <= SAFE_SCORE_BOUND

    l_ref[...] = jnp.zeros(l_ref.shape, F32)
    acc_ref[...] = jnp.zeros(acc_ref.shape, F32)

    def load_blocks(jb):
        off = jb * tk if isinstance(jb, int) else pl.multiple_of(jb * tk, tk)
        return k_ref[pl.ds(off, tk), :], vt_ref[:, pl.ds(off, tk)]

    def bounded_weights(jb, slot):
        kb, _ = load_blocks(jb)
        for c in range(2):
            st = lax.dot_general(kb, qc[c], nt_dims, preferred_element_type=F32)
            p = jnp.exp2(st - bound_c[c])
            l_ref[c] += jnp.sum(p, axis=0, keepdims=True)
            p_ref[slot, c] = p.astype(BF16)

    def weighted_values(jb, slot):
        _, vtb = load_blocks(jb)
        for c in range(2):
            acc_ref[c] += jnp.dot(vtb, p_ref[slot, c], preferred_element_type=F32)

    def bounded_pair(m, carry):
        j0 = 2 * m
        bounded_weights(j0 + 1, 1)
        weighted_values(j0, 0)
        bounded_weights(j0 + 2, 0)
        weighted_values(j0 + 1, 1)
        return carry

    def running_max_step(jb, carry):
        kb, vtb = load_blocks(jb)
        for c in range(2):
            st = lax.dot_general(kb, qc[c], nt_dims, preferred_element_type=F32)
            m_old = m_ref[c]
            m_new = jnp.maximum(m_old, jnp.max(st, axis=0, keepdims=True))
            alpha = jnp.exp2(m_old - m_new)
            p = jnp.exp2(st - m_new)
            l_ref[c] = alpha * l_ref[c] + jnp.sum(p, axis=0, keepdims=True)
            acc_ref[c] = alpha * acc_ref[c] + jnp.dot(vtb, p.astype(BF16), preferred_element_type=F32)
            m_ref[c] = m_new
        return carry

    @pl.when(use_bound)
    def _():
        bounded_weights(0, 0)
        lax.fori_loop(0, nk // 2 - 1, bounded_pair, 0)
        bounded_weights(nk - 1, 1)
        weighted_values(nk - 2, 0)
        weighted_values(nk - 1, 1)

    @pl.when(jnp.logical_not(use_bound))
    def _():
        m_ref[...] = jnp.full(m_ref.shape, -jnp.inf, F32)
        lax.fori_loop(0, nk, running_max_step, 0)

    lq1, lk1, lq2, lk2 = lam_ref[0:1, :], lam_ref[1:2, :], lam_ref[2:3, :], lam_ref[3:4, :]
    lam = (jnp.exp(jnp.sum(lq1 * lk1, axis=-1, keepdims=True))
           - jnp.exp(jnp.sum(lq2 * lk2, axis=-1, keepdims=True)) + lam_init)
    ot = acc_ref[0] / l_ref[0] - lam * (acc_ref[1] / l_ref[1])
    ms = jnp.mean(ot * ot, axis=0, keepdims=True)
    on = (ot * lax.rsqrt(ms + EPS)).T
    o_ref[...] = (on * sg_ref[...] * (1.0 - lam_init)).astype(BF16)


def _attention(proj, tok_off, batch, s_len, lam_pack, subln_g, *, tq, tk, lam_init):
    qb0 = tok_off // tq
    sb0 = tok_off // s_len
    nq = s_len // tq
    return pl.pallas_call(
        functools.partial(_attn_kernel, tk=tk, lam_init=lam_init),
        grid=(batch, N_HEADS, nq),
        in_specs=[
            pl.BlockSpec((tq, HEAD_W), lambda b, h, i: (qb0 + b * nq + i, COL_Q * N_HEADS + h)),
            pl.BlockSpec((s_len, HEAD_W), lambda b, h, i: (sb0 + b, COL_K * N_HEADS + h)),
            pl.BlockSpec((s_len, HEAD_W), lambda b, h, i: (sb0 + b, COL_V * N_HEADS + h)),
            pl.BlockSpec((8, HEAD_DIM), lambda b, h, i: (0, 0)),
            pl.BlockSpec((1, HEAD_W), lambda b, h, i: (0, 0)),
        ],
        out_specs=pl.BlockSpec((tq, HEAD_W), lambda b, h, i: (b * nq + i, h)),
        out_shape=jax.ShapeDtypeStruct((batch * s_len, ATTN_W), BF16),
        scratch_shapes=[
            pltpu.VMEM((HEAD_W, s_len), BF16),
            pltpu.VMEM((2, HEAD_W, tq), F32),
            pltpu.VMEM((2, 1, tq), F32),
            pltpu.VMEM((2, 1, tq), F32),
            pltpu.VMEM((8, LANES), F32),
            pltpu.VMEM((2, 2, tk, tq), BF16),
        ],
        compiler_params=_cparams(("arbitrary", "arbitrary", "arbitrary")),
        name="diff_attention",
    )(proj, proj, proj, lam_pack, subln_g)


def _post_kernel(pos_ref, len_ref, x0_ref, x1_ref, o0_ref, o1_ref, pin_ref, prev_ref, next_ref,
                 ga_ref, gp_ref, wau_ref, wgrp_ref, pscale_ref, wpu_ref, wout_ref, gffn_ref,
                 wrc_ref, br_ref, xnew_ref, ids_ref, wts_ref, *, nt0):
    i = pl.program_id(0)
    tm = pin_ref.shape[0]
    p0 = pos_ref[i]
    s_len = len_ref[i]
    first = i < nt0

    o = jnp.where(first, o0_ref[...], o1_ref[...])
    attn_d = jnp.dot(o, wau_ref[...], preferred_element_type=F32)

    ext = jnp.concatenate([prev_ref[...], pin_ref[...], next_ref[...]], axis=0)
    ke = tm + 2 * POOL_HALO
    r_band = lax.broadcasted_iota(I32, (tm, ke), 0) + p0
    c_band = lax.broadcasted_iota(I32, (tm, ke), 1) + (p0 - POOL_HALO)
    r_cnt = lax.broadcasted_iota(I32, (tm, POOL_GROUP_W), 0) + p0
    pooled = []
    for gi, w in enumerate(POOL_WINDOWS):
        back, fwd = w // 2, w - 1 - w // 2
        lo = jnp.maximum(r_band - back, 0)
        hi = jnp.minimum(r_band + fwd, s_len - 1)
        band = jnp.where((c_band >= lo) & (c_band <= hi), 1.0, 0.0).astype(BF16)
        cnt = jnp.minimum(r_cnt + fwd, s_len - 1) - jnp.maximum(r_cnt - back, 0) + 1
        cols = slice(gi * POOL_GROUP_W, (gi + 1) * POOL_GROUP_W)
        wsum = jnp.dot(band, ext[:, cols], preferred_element_type=F32)
        y = wsum / cnt.astype(F32) - pin_ref[:, cols].astype(F32)
        pooled.append(jnp.dot(y.astype(BF16), wgrp_ref[gi], preferred_element_type=F32))
    pooled = (jnp.concatenate(pooled, axis=1) * pscale_ref[...]).astype(BF16)
    pool_d = jnp.dot(pooled, wpu_ref[...], preferred_element_type=F32)

    merged = ga_ref[...].astype(F32) * attn_d + gp_ref[...].astype(F32) * pool_d
    x = jnp.where(first, x0_ref[...], x1_ref[...])
    xnew = x + jnp.dot(merged.astype(BF16), wout_ref[...], preferred_element_type=F32)
    _store_token_rows(xnew_ref, xnew)

    ms = jnp.mean(xnew * xnew, axis=-1, keepdims=True)
    hn = xnew * lax.rsqrt(ms + EPS) * gffn_ref[...]
    hi_part = hn.astype(BF16)
    lo_part = (hn - hi_part.astype(F32)).astype(BF16)
    hw = jnp.dot(hi_part, wrc_ref[...], preferred_element_type=F32)
    logits = (hw[:, 0:ROUTER_COLS]
              + (hw[:, ROUTER_COLS:]
                 + jnp.dot(lo_part, wrc_ref[:, 0:ROUTER_COLS], preferred_element_type=F32))
              + br_ref[...])
    lt = logits.T

    gl = lt[0:N_GROUPS, :]
    g_iota = lax.broadcasted_iota(I32, gl.shape, 0)
    g_max = jnp.max(gl, axis=0, keepdims=True)
    g_sel = jnp.min(jnp.where(gl == g_max, g_iota, N_GROUPS), axis=0, keepdims=True)
    g_w = 1.0 / jnp.sum(jnp.exp(gl - g_max), axis=0, keepdims=True)
    el = lt[EXPERT_COL0:EXPERT_COL0 + N_EXPERTS, :]
    e_iota = lax.broadcasted_iota(I32, el.shape, 0)
    neg = jnp.float32(-jnp.inf)
    el = jnp.where(e_iota // EXPERTS_PER_GROUP == g_sel, el, neg)
    v1 = jnp.max(el, axis=0, keepdims=True)
    i1 = jnp.min(jnp.where(el == v1, e_iota, N_EXPERTS), axis=0, keepdims=True)
    el2 = jnp.where(e_iota == i1, neg, el)
    v2 = jnp.max(el2, axis=0, keepdims=True)
    i2 = jnp.min(jnp.where(el2 == v2, e_iota, N_EXPERTS), axis=0, keepdims=True)
    d = jnp.exp(v2 - v1)
    w1 = g_w / (1.0 + d)
    w2 = g_w * d / (1.0 + d)
    row = lax.broadcasted_iota(I32, (8, tm), 0)
    ids_ref[...] = jnp.where(row == 0, i1, jnp.where(row == 1, i2, 0))
    wts_ref[...] = jnp.where(row == 0, w1, jnp.where(row == 1, w2, 0.0))


def _post_block(x0, x1, o0, o1, proj, tile_pos, tile_len, wau, wgrp, pscale, wpu, wout, gffn,
                wrc, br, *, tm):
    t0, t1 = x0.shape[0], x1.shape[0]
    t = t0 + t1
    nt0 = t0 // tm
    nt = t // tm
    hb = tm // POOL_HALO
    n_hblk = t // POOL_HALO
    const = lambda *shape: pl.BlockSpec(shape, lambda i, tp, tl: (0,) * len(shape),
                                        pipeline_mode=pl.Buffered(1))
    grid_spec = pltpu.PrefetchScalarGridSpec(
        num_scalar_prefetch=2,
        grid=(nt,),
        in_specs=[
            pl.BlockSpec((tm, D_MODEL), lambda i, tp, tl: (jnp.minimum(i, nt0 - 1), 0)),
            pl.BlockSpec((tm, D_MODEL), lambda i, tp, tl: (jnp.maximum(i - nt0, 0), 0)),
            pl.BlockSpec((tm, ATTN_W), lambda i, tp, tl: (jnp.minimum(i, nt0 - 1), 0)),
            pl.BlockSpec((tm, ATTN_W), lambda i, tp, tl: (jnp.maximum(i - nt0, 0), 0)),
            pl.BlockSpec((tm, POOL_W), lambda i, tp, tl: (i, COL_PIN)),
            pl.BlockSpec((POOL_HALO, POOL_W), lambda i, tp, tl: (jnp.maximum(i * hb - 1, 0), COL_PIN)),
            pl.BlockSpec((POOL_HALO, POOL_W), lambda i, tp, tl: (jnp.minimum((i + 1) * hb, n_hblk - 1), COL_PIN)),
            pl.BlockSpec((tm, D_MODEL), lambda i, tp, tl: (i, COL_GA // 2)),
            pl.BlockSpec((tm, D_MODEL), lambda i, tp, tl: (i, COL_GP // 2)),
            const(ATTN_W, D_MODEL),
            const(len(POOL_WINDOWS), POOL_GROUP_W, POOL_GROUP_W),
            const(1, POOL_W),
            const(POOL_W, D_MODEL),
            const(D_MODEL, D_MODEL),
            const(1, D_MODEL),
            const(D_MODEL, 2 * ROUTER_COLS),
            const(1, ROUTER_COLS),
        ],
        out_specs=[
            pl.BlockSpec((tm * ROW_PITCH, LANES), lambda i, tp, tl: (i, 0)),
            pl.BlockSpec((8, tm), lambda i, tp, tl: (0, i)),
            pl.BlockSpec((8, tm), lambda i, tp, tl: (0, i)),
        ],
    )
    return pl.pallas_call(
        functools.partial(_post_kernel, nt0=nt0),
        grid_spec=grid_spec,
        out_shape=[
            jax.ShapeDtypeStruct((t * ROW_PITCH, LANES), F32),
            jax.ShapeDtypeStruct((8, t), I32),
            jax.ShapeDtypeStruct((8, t), F32),
        ],
        compiler_params=_cparams(("arbitrary",)),
        name="post_block",
    )(tile_pos, tile_len, x0, x1, o0, o1, proj, proj, proj, proj, proj, wau, wgrp, pscale, wpu,
      wout, gffn, wrc, br)


GATHER_UNROLL = 8
WEIGHT_DMA_PRIORITY = 1


def _row_gather_start(src_hbm, idx_ref, base, n_rows, buf, sem, priority=0):
    def body(r, carry):
        src0 = idx_ref[base + r] * ROW_PITCH
        dst0 = r * ROW_PITCH
        pltpu.make_async_copy(src_hbm.at[pl.ds(src0, ROW_TILES)], buf.at[pl.ds(dst0, ROW_TILES)],
                              sem).start(priority=priority)
        return carry
    lax.fori_loop(0, n_rows, body, 0, unroll=GATHER_UNROLL)


def _row_gather_wait(src_hbm, n_rows, buf, sem):
    n = n_rows * ROW_TILES
    pltpu.make_async_copy(src_hbm.at[pl.ds(0, n)], buf.at[pl.ds(0, n)], sem).wait()


def _moe_kernel(te_ref, tf_ref, tn_ref, ts_ref, nu_ref, src_ref, x_hbm, gffn_ref, wg_hbm, wu_hbm, wd_hbm,
                y_ref, xbuf, sem, wgf, wuf, wdf, wsem, wgb, wub, wdb):
    i = pl.program_id(0)
    tm = y_ref.shape[0] // ROW_PITCH
    n_used = nu_ref[0]
    slot = lax.rem(i, 2)

    def weight_copies(e, s):
        return (pltpu.make_async_copy(wg_hbm.at[e], wgf.at[s], wsem.at[s]),
                pltpu.make_async_copy(wu_hbm.at[e], wuf.at[s], wsem.at[s]),
                pltpu.make_async_copy(wd_hbm.at[e], wdf.at[s], wsem.at[s]))

    @pl.when(i == 0)
    def _():
        _row_gather_start(x_hbm, src_ref, 0, tm, xbuf.at[0], sem.at[0])
        for cp in weight_copies(te_ref[0], 0):
            cp.start(priority=WEIGHT_DMA_PRIORITY)

    @pl.when(i + 1 < n_used)
    def _():
        _row_gather_start(x_hbm, src_ref, (i + 1) * tm, tm, xbuf.at[1 - slot], sem.at[1 - slot])

    @pl.when(i < n_used)
    def _():
        @pl.when(tf_ref[i] == 1)
        def _():
            ws = ts_ref[i]
            for cp in weight_copies(te_ref[i], ws):
                cp.wait()
            for s in range(2):
                @pl.when(ws == s)
                def _(s=s):
                    wgb[...] = wgf[s].astype(BF16)
                    wub[...] = wuf[s].astype(BF16)
                    wdb[...] = wdf[s].astype(BF16)

            @pl.when(tn_ref[i] >= 0)
            def _():
                for cp in weight_copies(tn_ref[i], 1 - ws):
                    cp.start(priority=WEIGHT_DMA_PRIORITY)

        _row_gather_wait(x_hbm, tm, xbuf.at[slot], sem.at[slot])
        xs = _load_token_rows(xbuf.at[slot], tm)
        ms = jnp.mean(xs * xs, axis=-1, keepdims=True)
        hn = (xs * lax.rsqrt(ms + EPS) * gffn_ref[...]).astype(BF16)
        gate = jnp.dot(hn, wgb[...], preferred_element_type=F32)
        up = jnp.dot(hn, wub[...], preferred_element_type=F32)
        h = jax.nn.silu(gate) * up
        _store_token_rows(y_ref, jnp.dot(h.astype(BF16), wdb[...], preferred_element_type=F32))

    @pl.when(i >= n_used)
    def _():
        y_ref[...] = jnp.zeros(y_ref.shape, y_ref.dtype)


def _moe_ffn(xnew, tile_e, tile_first, tile_next, tile_wslot, n_used, src_row, gffn, w_gate, w_up, w_down,
             *, tm):
    p_rows = src_row.shape[0]
    n_tiles = p_rows // tm
    grid_spec = pltpu.PrefetchScalarGridSpec(
        num_scalar_prefetch=6,
        grid=(n_tiles,),
        in_specs=[
            pl.BlockSpec(memory_space=pl.ANY),
            pl.BlockSpec((1, D_MODEL), lambda i, *_: (0, 0)),
            pl.BlockSpec(memory_space=pl.ANY),
            pl.BlockSpec(memory_space=pl.ANY),
            pl.BlockSpec(memory_space=pl.ANY),
        ],
        out_specs=pl.BlockSpec((tm * ROW_PITCH, LANES), lambda i, *_: (i, 0)),
        scratch_shapes=[
            pltpu.VMEM((2, tm * ROW_PITCH, LANES), F32),
            pltpu.SemaphoreType.DMA((2,)),
            pltpu.VMEM((2, D_MODEL, D_EXPERT), F32),
            pltpu.VMEM((2, D_MODEL, D_EXPERT), F32),
            pltpu.VMEM((2, D_EXPERT, D_MODEL), F32),
            pltpu.SemaphoreType.DMA((2,)),
            pltpu.VMEM((D_MODEL, D_EXPERT), BF16),
            pltpu.VMEM((D_MODEL, D_EXPERT), BF16),
            pltpu.VMEM((D_EXPERT, D_MODEL), BF16),
        ],
    )
    return pl.pallas_call(
        _moe_kernel,
        grid_spec=grid_spec,
        out_shape=jax.ShapeDtypeStruct((p_rows * ROW_PITCH, LANES), F32),
        compiler_params=_cparams(("arbitrary",)),
        name="moe_ffn",
    )(tile_e, tile_first, tile_next, tile_wslot, n_used, src_row, xnew, gffn, w_gate, w_up, w_down)


def _combine_kernel(pos_ref, xnew_ref, wts_ref, y_hbm, out0_ref, out1_ref, ybuf, sem, *, nt0, n_tok):
    i = pl.program_id(0)
    nt = pl.num_programs(0)
    tm = xnew_ref.shape[0] // ROW_PITCH
    slot = lax.rem(i, 2)

    def start(tile, s):
        _row_gather_start(y_hbm, pos_ref, tile * tm, tm, ybuf.at[s, 0], sem.at[s])
        _row_gather_start(y_hbm, pos_ref, n_tok + tile * tm, tm, ybuf.at[s, 1], sem.at[s], priority=1)

    @pl.when(i == 0)
    def _():
        start(0, 0)

    @pl.when(i + 1 < nt)
    def _():
        start(i + 1, 1 - slot)

    _row_gather_wait(y_hbm, tm, ybuf.at[slot, 0], sem.at[slot])
    _row_gather_wait(y_hbm, tm, ybuf.at[slot, 1], sem.at[slot])
    wt = wts_ref[...].T
    w1 = jnp.broadcast_to(wt[:, 0:1], (tm, LANES))
    w2 = jnp.broadcast_to(wt[:, 1:2], (tm, LANES))

    def write(out_ref):
        for s in range(ROW_TILES):
            out_ref[:, s * LANES:(s + 1) * LANES] = (
                _token_slab(xnew_ref, tm, s)
                + (w1 * _token_slab(ybuf.at[slot, 0], tm, s) + w2 * _token_slab(ybuf.at[slot, 1], tm, s)))

    @pl.when(i < nt0)
    def _():
        write(out0_ref)

    @pl.when(i >= nt0)
    def _():
        write(out1_ref)


def _combine(xnew, y_sorted, wts, pos, t0, *, tm):
    t = xnew.shape[0] // ROW_PITCH
    nt0 = t0 // tm
    nt = t // tm
    grid_spec = pltpu.PrefetchScalarGridSpec(
        num_scalar_prefetch=1,
        grid=(nt,),
        in_specs=[
            pl.BlockSpec((tm * ROW_PITCH, LANES), lambda i, ps: (i, 0)),
            pl.BlockSpec((8, tm), lambda i, ps: (0, i)),
            pl.BlockSpec(memory_space=pl.ANY),
        ],
        out_specs=[
            pl.BlockSpec((tm, D_MODEL), lambda i, ps: (jnp.minimum(i, nt0 - 1), 0)),
            pl.BlockSpec((tm, D_MODEL), lambda i, ps: (jnp.maximum(i - nt0, 0), 0)),
        ],
        scratch_shapes=[
            pltpu.VMEM((2, 2, tm * ROW_PITCH, LANES), F32),
            pltpu.SemaphoreType.DMA((2,)),
        ],
    )
    return pl.pallas_call(
        functools.partial(_combine_kernel, nt0=nt0, n_tok=t),
        grid_spec=grid_spec,
        out_shape=[
            jax.ShapeDtypeStruct((t0, D_MODEL), F32),
            jax.ShapeDtypeStruct((t - t0, D_MODEL), F32),
        ],
        compiler_params=_cparams(("arbitrary",)),
        name="moe_combine",
    )(pos, xnew, wts, y_sorted)


def _rope_tables(max_len):
    inv = 1.0 / (ROPE_THETA ** (jnp.arange(0, HEAD_DIM, 2, dtype=F32) / HEAD_DIM))
    ang = jnp.arange(max_len, dtype=F32)[:, None] * inv[None, :]
    ang = jnp.concatenate([ang, ang, ang, ang], axis=-1)
    lane = jnp.arange(LANES)
    sign = jnp.where((lane % HEAD_DIM) < HEAD_DIM // 2, -1.0, 1.0).astype(F32)
    return jnp.cos(ang), jnp.sin(ang) * sign[None, :]


def _tile_tables(groups, tm):
    pos, length = [], []
    for batch, s_len in groups:
        for _ in range(batch):
            for p in range(0, s_len, tm):
                pos.append(p)
                length.append(s_len)
    return np.asarray(pos, np.int32), np.asarray(length, np.int32)


def _routing_tables(ids, n_tok, tm):
    e_flat = ids[0:2, :].reshape(-1)
    onehot = (e_flat[:, None] == jnp.arange(N_EXPERTS, dtype=I32)[None, :]).astype(I32)
    csum = jnp.cumsum(onehot, axis=0)
    rank = jnp.sum(csum * onehot, axis=1) - 1
    counts = csum[-1]
    padded = ((counts + tm - 1) // tm) * tm
    ends = jnp.cumsum(padded)
    starts = ends - padded
    dst = (starts[e_flat] + rank).astype(I32)
    p_rows = 2 * n_tok + N_EXPERTS * tm
    n_tiles = p_rows // tm
    tok = jnp.tile(jnp.arange(n_tok, dtype=I32), 2)
    src_row = jnp.zeros((p_rows,), I32).at[dst].set(tok)
    n_used = (ends[-1] // tm).astype(I32)
    tile_start = jnp.arange(n_tiles, dtype=I32) * tm
    tile_e = jnp.sum((ends[None, :] <= tile_start[:, None]).astype(I32), axis=1)
    tile_e = jnp.minimum(tile_e, N_EXPERTS - 1)
    active = jnp.arange(n_tiles, dtype=I32) < n_used
    last_e = tile_e[jnp.maximum(n_used - 1, 0)]
    tile_e = jnp.where(active, tile_e, last_e)
    tile_first = (active & (tile_start == starts[tile_e])).astype(I32)
    tile_wslot = lax.rem(jnp.cumsum(tile_first) - 1, 2).astype(I32)
    e_ids = jnp.arange(N_EXPERTS, dtype=I32)
    later_used = (counts[None, :] > 0) & (e_ids[None, :] > e_ids[:, None])
    next_used = jnp.min(jnp.where(later_used, e_ids[None, :], N_EXPERTS), axis=1)
    next_used = jnp.where(next_used == N_EXPERTS, -1, next_used).astype(I32)
    tile_next = next_used[tile_e]
    return tile_e, tile_first, tile_next, tile_wslot, n_used.reshape(1), src_row, dst


def _forward(x0, x1, groups, params, *, tm_proj, tn_proj, rc_proj, tq, tk, tm_post, tm_moe, tm_comb):
    (attn_norm_g, w_in, q_norm_g, k_norm_g, lambda_q1, lambda_k1, lambda_q2, lambda_k2, subln_g,
     w_attn_up, w_pool_grp, pool_scale, w_pool_up, w_out, ffn_norm_g, w_group_router,
     b_group_router, w_expert_router, b_expert_router, w_gate, w_up, w_down) = params
    t0, t1 = x0.shape[0], x1.shape[0]
    n_tok = t0 + t1
    lam_init = 0.8 - 0.6 * math.exp(-0.3 * 0)
    max_len = max(s for _, s in groups)

    cos_t, sin_t = _rope_tables(max_len)
    pos_proj, _ = _tile_tables(groups, tm_proj)
    q_scale = HEAD_DIM ** -0.5 * math.log2(math.e)
    qkg = jnp.zeros((8, ATTN_W), F32)
    qkg = qkg.at[0].set(jnp.tile(q_norm_g, ATTN_W // HEAD_DIM) * q_scale)
    qkg = qkg.at[1].set(jnp.tile(k_norm_g, ATTN_W // HEAD_DIM))
    seg = np.arange(LANES) // HEAD_DIM
    bd = jnp.asarray((seg[:, None] == seg[None, :]).astype(np.float32) / HEAD_DIM, BF16)

    proj = _in_projection(x0, x1, jnp.asarray(pos_proj // tm_proj), attn_norm_g[None, :],
                          w_in.astype(BF16), cos_t, sin_t, qkg, bd, tm=tm_proj, tn=tn_proj, rc=rc_proj)

    lam_pack = jnp.zeros((8, HEAD_DIM), F32)
    lam_pack = lam_pack.at[0].set(lambda_q1).at[1].set(lambda_k1).at[2].set(lambda_q2).at[3].set(lambda_k2)
    outs = []
    tok_off = 0
    for batch, s_len in groups:
        outs.append(_attention(proj, tok_off, batch, s_len, lam_pack, subln_g[None, :],
                               tq=min(tq, s_len), tk=min(tk, s_len), lam_init=lam_init))
        tok_off += batch * s_len

    tile_pos, tile_len = _tile_tables(groups, tm_post)
    w_router = jnp.zeros((D_MODEL, ROUTER_COLS), F32)
    w_router = w_router.at[:, 0:N_GROUPS].set(w_group_router)
    w_router = w_router.at[:, EXPERT_COL0:EXPERT_COL0 + N_EXPERTS].set(w_expert_router)
    b_router = jnp.zeros((1, ROUTER_COLS), F32)
    b_router = b_router.at[0, 0:N_GROUPS].set(b_group_router)
    b_router = b_router.at[0, EXPERT_COL0:EXPERT_COL0 + N_EXPERTS].set(b_expert_router)
    wr_hi = w_router.astype(BF16)
    wr_lo = (w_router - wr_hi.astype(F32)).astype(BF16)
    xnew, ids, wts = _post_block(
        x0, x1, outs[0], outs[1], proj, jnp.asarray(tile_pos), jnp.asarray(tile_len),
        w_attn_up.astype(BF16), w_pool_grp.astype(BF16), pool_scale[None, :], w_pool_up.astype(BF16),
        w_out.astype(BF16), ffn_norm_g[None, :], jnp.concatenate([wr_hi, wr_lo], axis=1), b_router,
        tm=tm_post)

    tile_e, tile_first, tile_next, tile_wslot, n_used, src_row, dst = _routing_tables(ids, n_tok, tm_moe)
    y_sorted = _moe_ffn(
        xnew, tile_e, tile_first, tile_next, tile_wslot, n_used, src_row, ffn_norm_g[None, :],
        w_gate.reshape(N_EXPERTS, D_MODEL, D_EXPERT), w_up.reshape(N_EXPERTS, D_MODEL, D_EXPERT),
        w_down.reshape(N_EXPERTS, D_EXPERT, D_MODEL), tm=tm_moe)
    return _combine(xnew, y_sorted, wts, dst, t0, tm=tm_comb)


def kernel(x_prompt, x_sample, attn_norm_g, w_in, q_norm_g, k_norm_g, lambda_q1, lambda_k1, lambda_q2,
           lambda_k2, subln_g, w_attn_up, w_pool_grp, pool_scale, w_pool_up, w_out, ffn_norm_g,
           w_group_router, b_group_router, w_expert_router, b_expert_router, w_gate, w_up, w_down):
    assert attn_norm_g.shape[0] == 1, "single-layer stack"
    params = tuple(p[0] for p in (
        attn_norm_g, w_in, q_norm_g, k_norm_g, lambda_q1, lambda_k1, lambda_q2, lambda_k2, subln_g,
        w_attn_up, w_pool_grp, pool_scale, w_pool_up, w_out, ffn_norm_g, w_group_router,
        b_group_router, w_expert_router, b_expert_router, w_gate, w_up, w_down))
    groups = (x_prompt.shape[:2], x_sample.shape[:2])
    y0, y1 = _forward(
        x_prompt.reshape(-1, D_MODEL), x_sample.reshape(-1, D_MODEL), groups, params,
        tm_proj=1024, tn_proj=1024, rc_proj=512, tq=1024, tk=512, tm_post=256, tm_moe=256, tm_comb=256)
    return y0.reshape(x_prompt.shape), y1.reshape(x_sample.shape)
```

```python
import functools
import math

import numpy as np
import jax
import jax.numpy as jnp
from jax import lax
from jax.experimental import pallas as pl
from jax.experimental.pallas import tpu as pltpu

F32 = jnp.float32
BF16 = jnp.bfloat16
I32 = jnp.int32

D_MODEL = 2048
HEAD_DIM = 64
N_HEADS = 8
HEAD_W = 2 * HEAD_DIM
ATTN_W = N_HEADS * HEAD_W
POOL_W = 1024
POOL_WINDOWS = (2, 4, 8, 16)
POOL_GROUP_W = POOL_W // len(POOL_WINDOWS)
POOL_HALO = 64
IN_COLS = 3 * ATTN_W + POOL_W + 2 * D_MODEL
N_GROUPS = 4
EXPERTS_PER_GROUP = 8
N_EXPERTS = N_GROUPS * EXPERTS_PER_GROUP
D_EXPERT = 512
ROPE_THETA = 10000.0
EPS = 1e-6
LANES = 128
ROUTER_COLS = 128
EXPERT_COL0 = 8
ROW_TILES = D_MODEL // LANES
ROW_PITCH = ROW_TILES + 1
SAFE_SCORE_BOUND = 50.0
SCORE_BOUND_MARGIN = 1.02

COL_Q, COL_K, COL_V, COL_PIN, COL_GA, COL_GP = 0, 1, 2, 3, 4, 6

VMEM_LIMIT = 56 * 1024 * 1024


def _cparams(sem, vmem=VMEM_LIMIT):
    return pltpu.CompilerParams(dimension_semantics=sem, vmem_limit_bytes=vmem)


def _token_slab(ref, n_tok, s):
    return ref[pl.ds(s, n_tok, stride=ROW_PITCH), :]


def _load_token_rows(ref, n_tok):
    return jnp.concatenate([_token_slab(ref, n_tok, s) for s in range(ROW_TILES)], axis=1)


def _store_token_rows(ref, value):
    n_tok = value.shape[0]
    for s in range(ROW_TILES):
        ref[pl.ds(s, n_tok, stride=ROW_PITCH), :] = value[:, s * LANES:(s + 1) * LANES]
    ref[pl.ds(ROW_TILES, n_tok, stride=ROW_PITCH), :] = jnp.zeros((n_tok, LANES), ref.dtype)


def _inproj_kernel(posblk_ref, x0_hbm, x1_hbm, gattn_ref, w_ref, cos_ref, sin_ref, qkg_ref, bd_ref,
                   o_ref, xn_ref, xbuf, xsem, *, nt0, rc):
    del posblk_ref
    i = pl.program_id(0)
    j = pl.program_id(1)
    nt = pl.num_programs(0)
    tm = xn_ref.shape[0]
    n_chunks = tm // rc

    def x_tile_copy(x_hbm, tile):
        return pltpu.make_async_copy(x_hbm.at[pl.ds(tile * tm, tm)], xbuf, xsem)

    def start_x_tile(tile):
        @pl.when(tile < nt0)
        def _():
            x_tile_copy(x0_hbm, tile).start()

        @pl.when(tile >= nt0)
        def _():
            x_tile_copy(x1_hbm, tile - nt0).start()

    @pl.when(j == 0)
    def _():
        @pl.when(i == 0)
        def _():
            start_x_tile(i)

        x_tile_copy(x0_hbm, 0).wait()
        for c in range(n_chunks):
            xs = xbuf[c * rc:(c + 1) * rc, :]
            ms = jnp.mean(xs * xs, axis=-1, keepdims=True)
            xn_ref[c * rc:(c + 1) * rc, :] = (xs * lax.rsqrt(ms + EPS) * gattn_ref[...]).astype(BF16)

        @pl.when(i + 1 < nt)
        def _():
            start_x_tile(i + 1)

    def chunk_acc(c):
        return jnp.dot(xn_ref[c * rc:(c + 1) * rc, :], w_ref[...], preferred_element_type=F32)

    @pl.when(j <= COL_K)
    def _():
        g = jnp.where(j == COL_Q, qkg_ref[0:1, :], qkg_ref[1:2, :])
        lane = lax.broadcasted_iota(I32, (rc, LANES), 1)
        first_half = (lane % HEAD_DIM) < (HEAD_DIM // 2)
        for c in range(n_chunks):
            a = chunk_acc(c)
            cos = cos_ref[c * rc:(c + 1) * rc, :]
            sin = sin_ref[c * rc:(c + 1) * rc, :]
            for t in range(a.shape[1] // LANES):
                at = a[:, t * LANES:(t + 1) * LANES]
                ms = jnp.dot((at * at).astype(BF16), bd_ref[...], preferred_element_type=F32)
                u = at * g[:, t * LANES:(t + 1) * LANES]
                rot = jnp.where(first_half, pltpu.roll(u, LANES - HEAD_DIM // 2, 1),
                                pltpu.roll(u, HEAD_DIM // 2, 1))
                val = (u * cos + rot * sin) * lax.rsqrt(ms + EPS)
                o_ref[c * rc:(c + 1) * rc, t * LANES:(t + 1) * LANES] = val.astype(BF16)

    @pl.when((j == COL_V) | (j == COL_PIN))
    def _():
        for c in range(n_chunks):
            o_ref[c * rc:(c + 1) * rc, :] = chunk_acc(c).astype(BF16)

    @pl.when(j >= COL_GA)
    def _():
        for c in range(n_chunks):
            o_ref[c * rc:(c + 1) * rc, :] = jax.nn.sigmoid(chunk_acc(c)).astype(BF16)


def _in_projection(x0, x1, posblk, gattn, w_in, cos_t, sin_t, qkg, bd, *, tm, tn, rc):
    t0, t1 = x0.shape[0], x1.shape[0]
    nt0, nt1 = t0 // tm, t1 // tm
    nt = nt0 + nt1
    grid_spec = pltpu.PrefetchScalarGridSpec(
        num_scalar_prefetch=1,
        grid=(nt, IN_COLS // tn),
        in_specs=[
            pl.BlockSpec(memory_space=pl.ANY),
            pl.BlockSpec(memory_space=pl.ANY),
            pl.BlockSpec((1, D_MODEL), lambda i, j, pb: (0, 0)),
            pl.BlockSpec((D_MODEL, tn), lambda i, j, pb: (0, j)),
            pl.BlockSpec((tm, LANES), lambda i, j, pb: (pb[i], 0)),
            pl.BlockSpec((tm, LANES), lambda i, j, pb: (pb[i], 0)),
            pl.BlockSpec((8, tn), lambda i, j, pb: (0, 0)),
            pl.BlockSpec((LANES, LANES), lambda i, j, pb: (0, 0)),
        ],
        out_specs=pl.BlockSpec((tm, tn), lambda i, j, pb: (i, j)),
        scratch_shapes=[
            pltpu.VMEM((tm, D_MODEL), BF16),
            pltpu.VMEM((tm, D_MODEL), F32),
            pltpu.SemaphoreType.DMA(()),
        ],
    )
    return pl.pallas_call(
        functools.partial(_inproj_kernel, nt0=nt0, rc=rc),
        grid_spec=grid_spec,
        out_shape=jax.ShapeDtypeStruct((t0 + t1, IN_COLS), BF16),
        compiler_params=_cparams(("arbitrary", "arbitrary")),
        name="in_projection",
    )(posblk, x0, x1, gattn, w_in, cos_t, sin_t, qkg, bd)


def _attn_kernel(q_ref, k_ref, v_ref, lam_ref, sg_ref, o_ref, vt_ref, acc_ref, m_ref, l_ref, kn_ref,
                 p_ref, *, tk, lam_init):
    i = pl.program_id(2)
    s_len = k_ref.shape[0]
    tq = q_ref.shape[0]
    nk = s_len // tk
    nt_dims = (((1,), (1,)), ((), ()))

    sel_r = lax.broadcasted_iota(I32, (8, HEAD_W), 0)
    sel_l = lax.broadcasted_iota(I32, (8, HEAD_W), 1)
    sel = jnp.where(sel_l // HEAD_DIM == sel_r, 1.0, 0.0).astype(BF16)

    @pl.when(i == 0)
    def _():
        kn2 = jnp.zeros((8, 1), F32)
        for jb in range(nk):
            rows = slice(jb * tk, (jb + 1) * tk)
            vt_ref[:, rows] = v_ref[rows, :].astype(F32).T.astype(BF16)
            kf = k_ref[rows, :].astype(F32)
            n2 = lax.dot_general(sel, (kf * kf).astype(BF16), nt_dims, preferred_element_type=F32)
            kn2 = jnp.maximum(kn2, jnp.max(n2, axis=1, keepdims=True))
        kn_ref[...] = jnp.broadcast_to(kn2, kn_ref.shape)

    q = q_ref[...]
    lane = lax.broadcasted_iota(I32, q.shape, 1)
    zero = jnp.zeros_like(q)
    qc = (jnp.where(lane < HEAD_DIM, q, zero), jnp.where(lane >= HEAD_DIM, q, zero))

    qf = q.astype(F32)
    qn2 = lax.dot_general(sel, (qf * qf).astype(BF16), nt_dims, preferred_element_type=F32)
    bound = jnp.sqrt(qn2 * kn_ref[:, 0:1]) * SCORE_BOUND_MARGIN
    bound_c = (bound[0:1, :], bound[1:2, :])
    use_bound = jnp.max(bound[0:2, :]) <= SAFE_SCORE_BOUND

    l_ref[...] = jnp.zeros(l_ref.shape, F32)
    acc_ref[...] = jnp.zeros(acc_ref.shape, F32)

    def load_blocks(jb):
        off = jb * tk if isinstance(jb, int) else pl.multiple_of(jb * tk, tk)
        return k_ref[pl.ds(off, tk), :], vt_ref[:, pl.ds(off, tk)]

    def bounded_weights(jb, slot):
        kb, _ = load_blocks(jb)
        for c in range(2):
            st = lax.dot_general(kb, qc[c], nt_dims, preferred_element_type=F32)
            p = jnp.exp2(st - bound_c[c])
            l_ref[c] += jnp.sum(p, axis=0, keepdims=True)
            p_ref[slot, c] = p.astype(BF16)

    def weighted_values(jb, slot):
        _, vtb = load_blocks(jb)
        for c in range(2):
            acc_ref[c] += jnp.dot(vtb, p_ref[slot, c], preferred_element_type=F32)

    def bounded_pair(m, carry):
        j0 = 2 * m
        bounded_weights(j0 + 1, 1)
        weighted_values(j0, 0)
        bounded_weights(j0 + 2, 0)
        weighted_values(j0 + 1, 1)
        return carry

    def running_max_step(jb, carry):
        kb, vtb = load_blocks(jb)
        for c in range(2):
            st = lax.dot_general(kb, qc[c], nt_dims, preferred_element_type=F32)
            m_old = m_ref[c]
            m_new = jnp.maximum(m_old, jnp.max(st, axis=0, keepdims=True))
            alpha = jnp.exp2(m_old - m_new)
            p = jnp.exp2(st - m_new)
            l_ref[c] = alpha * l_ref[c] + jnp.sum(p, axis=0, keepdims=True)
            acc_ref[c] = alpha * acc_ref[c] + jnp.dot(vtb, p.astype(BF16), preferred_element_type=F32)
            m_ref[c] = m_new
        return carry

    @pl.when(use_bound)
    def _():
        bounded_weights(0, 0)
        lax.fori_loop(0, nk // 2 - 1, bounded_pair, 0)
        bounded_weights(nk - 1, 1)
        weighted_values(nk - 2, 0)
        weighted_values(nk - 1, 1)

    @pl.when(jnp.logical_not(use_bound))
    def _():
        m_ref[...] = jnp.full(m_ref.shape, -jnp.inf, F32)
        lax.fori_loop(0, nk, running_max_step, 0)

    lq1, lk1, lq2, lk2 = lam_ref[0:1, :], lam_ref[1:2, :], lam_ref[2:3, :], lam_ref[3:4, :]
    lam = (jnp.exp(jnp.sum(lq1 * lk1, axis=-1, keepdims=True))
           - jnp.exp(jnp.sum(lq2 * lk2, axis=-1, keepdims=True)) + lam_init)
    ot = acc_ref[0] / l_ref[0] - lam * (acc_ref[1] / l_ref[1])
    ms = jnp.mean(ot * ot, axis=0, keepdims=True)
    on = (ot * lax.rsqrt(ms + EPS)).T
    o_ref[...] = (on * sg_ref[...] * (1.0 - lam_init)).astype(BF16)


def _attention(proj, tok_off, batch, s_len, lam_pack, subln_g, *, tq, tk, lam_init):
    qb0 = tok_off // tq
    sb0 = tok_off // s_len
    nq = s_len // tq
    return pl.pallas_call(
        functools.partial(_attn_kernel, tk=tk, lam_init=lam_init),
        grid=(batch, N_HEADS, nq),
        in_specs=[
            pl.BlockSpec((tq, HEAD_W), lambda b, h, i: (qb0 + b * nq + i, COL_Q * N_HEADS + h)),
            pl.BlockSpec((s_len, HEAD_W), lambda b, h, i: (sb0 + b, COL_K * N_HEADS + h)),
            pl.BlockSpec((s_len, HEAD_W), lambda b, h, i: (sb0 + b, COL_V * N_HEADS + h)),
            pl.BlockSpec((8, HEAD_DIM), lambda b, h, i: (0, 0)),
            pl.BlockSpec((1, HEAD_W), lambda b, h, i: (0, 0)),
        ],
        out_specs=pl.BlockSpec((tq, HEAD_W), lambda b, h, i: (b * nq + i, h)),
        out_shape=jax.ShapeDtypeStruct((batch * s_len, ATTN_W), BF16),
        scratch_shapes=[
            pltpu.VMEM((HEAD_W, s_len), BF16),
            pltpu.VMEM((2, HEAD_W, tq), F32),
            pltpu.VMEM((2, 1, tq), F32),
            pltpu.VMEM((2, 1, tq), F32),
            pltpu.VMEM((8, LANES), F32),
            pltpu.VMEM((2, 2, tk, tq), BF16),
        ],
        compiler_params=_cparams(("arbitrary", "arbitrary", "arbitrary")),
        name="diff_attention",
    )(proj, proj, proj, lam_pack, subln_g)


def _post_kernel(pos_ref, len_ref, x0_ref, x1_ref, o0_ref, o1_ref, pin_ref, prev_ref, next_ref,
                 ga_ref, gp_ref, wau_ref, wgrp_ref, pscale_ref, wpu_ref, wout_ref, gffn_ref,
                 wrc_ref, br_ref, xnew_ref, ids_ref, wts_ref, *, nt0):
    i = pl.program_id(0)
    tm = pin_ref.shape[0]
    p0 = pos_ref[i]
    s_len = len_ref[i]
    first = i < nt0

    o = jnp.where(first, o0_ref[...], o1_ref[...])
    attn_d = jnp.dot(o, wau_ref[...], preferred_element_type=F32)

    ext = jnp.concatenate([prev_ref[...], pin_ref[...], next_ref[...]], axis=0)
    ke = tm + 2 * POOL_HALO
    r_band = lax.broadcasted_iota(I32, (tm, ke), 0) + p0
    c_band = lax.broadcasted_iota(I32, (tm, ke), 1) + (p0 - POOL_HALO)
    r_cnt = lax.broadcasted_iota(I32, (tm, POOL_GROUP_W), 0) + p0
    pooled = []
    for gi, w in enumerate(POOL_WINDOWS):
        back, fwd = w // 2, w - 1 - w // 2
        lo = jnp.maximum(r_band - back, 0)
        hi = jnp.minimum(r_band + fwd, s_len - 1)
        band = jnp.where((c_band >= lo) & (c_band <= hi), 1.0, 0.0).astype(BF16)
        cnt = jnp.minimum(r_cnt + fwd, s_len - 1) - jnp.maximum(r_cnt - back, 0) + 1
        cols = slice(gi * POOL_GROUP_W, (gi + 1) * POOL_GROUP_W)
        wsum = jnp.dot(band, ext[:, cols], preferred_element_type=F32)
        y = wsum / cnt.astype(F32) - pin_ref[:, cols].astype(F32)
        pooled.append(jnp.dot(y.astype(BF16), wgrp_ref[gi], preferred_element_type=F32))
    pooled = (jnp.concatenate(pooled, axis=1) * pscale_ref[...]).astype(BF16)
    pool_d = jnp.dot(pooled, wpu_ref[...], preferred_element_type=F32)

    merged = ga_ref[...].astype(F32) * attn_d + gp_ref[...].astype(F32) * pool_d
    x = jnp.where(first, x0_ref[...], x1_ref[...])
    xnew = x + jnp.dot(merged.astype(BF16), wout_ref[...], preferred_element_type=F32)
    _store_token_rows(xnew_ref, xnew)

    ms = jnp.mean(xnew * xnew, axis=-1, keepdims=True)
    hn = xnew * lax.rsqrt(ms + EPS) * gffn_ref[...]
    hi_part = hn.astype(BF16)
    lo_part = (hn - hi_part.astype(F32)).astype(BF16)
    hw = jnp.dot(hi_part, wrc_ref[...], preferred_element_type=F32)
    logits = (hw[:, 0:ROUTER_COLS]
              + (hw[:, ROUTER_COLS:]
                 + jnp.dot(lo_part, wrc_ref[:, 0:ROUTER_COLS], preferred_element_type=F32))
              + br_ref[...])
    lt = logits.T

    gl = lt[0:N_GROUPS, :]
    g_iota = lax.broadcasted_iota(I32, gl.shape, 0)
    g_max = jnp.max(gl, axis=0, keepdims=True)
    g_sel = jnp.min(jnp.where(gl == g_max, g_iota, N_GROUPS), axis=0, keepdims=True)
    g_w = 1.0 / jnp.sum(jnp.exp(gl - g_max), axis=0, keepdims=True)
    el = lt[EXPERT_COL0:EXPERT_COL0 + N_EXPERTS, :]
    e_iota = lax.broadcasted_iota(I32, el.shape, 0)
    neg = jnp.float32(-jnp.inf)
    el = jnp.where(e_iota // EXPERTS_PER_GROUP == g_sel, el, neg)
    v1 = jnp.max(el, axis=0, keepdims=True)
    i1 = jnp.min(jnp.where(el == v1, e_iota, N_EXPERTS), axis=0, keepdims=True)
    el2 = jnp.where(e_iota == i1, neg, el)
    v2 = jnp.max(el2, axis=0, keepdims=True)
    i2 = jnp.min(jnp.where(el2 == v2, e_iota, N_EXPERTS), axis=0, keepdims=True)
    d = jnp.exp(v2 - v1)
    w1 = g_w / (1.0 + d)
    w2 = g_w * d / (1.0 + d)
    row = lax.broadcasted_iota(I32, (8, tm), 0)
    ids_ref[...] = jnp.where(row == 0, i1, jnp.where(row == 1, i2, 0))
    wts_ref[...] = jnp.where(row == 0, w1, jnp.where(row == 1, w2, 0.0))


def _post_block(x0, x1, o0, o1, proj, tile_pos, tile_len, wau, wgrp, pscale, wpu, wout, gffn,
                wrc, br, *, tm):
    t0, t1 = x0.shape[0], x1.shape[0]
    t = t0 + t1
    nt0 = t0 // tm
    nt = t // tm
    hb = tm // POOL_HALO
    n_hblk = t // POOL_HALO
    const = lambda *shape: pl.BlockSpec(shape, lambda i, tp, tl: (0,) * len(shape),
                                        pipeline_mode=pl.Buffered(1))
    grid_spec = pltpu.PrefetchScalarGridSpec(
        num_scalar_prefetch=2,
        grid=(nt,),
        in_specs=[
            pl.BlockSpec((tm, D_MODEL), lambda i, tp, tl: (jnp.minimum(i, nt0 - 1), 0)),
            pl.BlockSpec((tm, D_MODEL), lambda i, tp, tl: (jnp.maximum(i - nt0, 0), 0)),
            pl.BlockSpec((tm, ATTN_W), lambda i, tp, tl: (jnp.minimum(i, nt0 - 1), 0)),
            pl.BlockSpec((tm, ATTN_W), lambda i, tp, tl: (jnp.maximum(i - nt0, 0), 0)),
            pl.BlockSpec((tm, POOL_W), lambda i, tp, tl: (i, COL_PIN)),
            pl.BlockSpec((POOL_HALO, POOL_W), lambda i, tp, tl: (jnp.maximum(i * hb - 1, 0), COL_PIN)),
            pl.BlockSpec((POOL_HALO, POOL_W), lambda i, tp, tl: (jnp.minimum((i + 1) * hb, n_hblk - 1), COL_PIN)),
            pl.BlockSpec((tm, D_MODEL), lambda i, tp, tl: (i, COL_GA // 2)),
            pl.BlockSpec((tm, D_MODEL), lambda i, tp, tl: (i, COL_GP // 2)),
            const(ATTN_W, D_MODEL),
            const(len(POOL_WINDOWS), POOL_GROUP_W, POOL_GROUP_W),
            const(1, POOL_W),
            const(POOL_W, D_MODEL),
            const(D_MODEL, D_MODEL),
            const(1, D_MODEL),
            const(D_MODEL, 2 * ROUTER_COLS),
            const(1, ROUTER_COLS),
        ],
        out_specs=[
            pl.BlockSpec((tm * ROW_PITCH, LANES), lambda i, tp, tl: (i, 0)),
            pl.BlockSpec((8, tm), lambda i, tp, tl: (0, i)),
            pl.BlockSpec((8, tm), lambda i, tp, tl: (0, i)),
        ],
    )
    return pl.pallas_call(
        functools.partial(_post_kernel, nt0=nt0),
        grid_spec=grid_spec,
        out_shape=[
            jax.ShapeDtypeStruct((t * ROW_PITCH, LANES), F32),
            jax.ShapeDtypeStruct((8, t), I32),
            jax.ShapeDtypeStruct((8, t), F32),
        ],
        compiler_params=_cparams(("arbitrary",)),
        name="post_block",
    )(tile_pos, tile_len, x0, x1, o0, o1, proj, proj, proj, proj, proj, wau, wgrp, pscale, wpu,
      wout, gffn, wrc, br)


GATHER_UNROLL = 8
WEIGHT_DMA_PRIORITY = 1


def _row_gather_start(src_hbm, idx_ref, base, n_rows, buf, sem, priority=0):
    def body(r, carry):
        src0 = idx_ref[base + r] * ROW_PITCH
        dst0 = r * ROW_PITCH
        pltpu.make_async_copy(src_hbm.at[pl.ds(src0, ROW_TILES)], buf.at[pl.ds(dst0, ROW_TILES)],
                              sem).start(priority=priority)
        return carry
    lax.fori_loop(0, n_rows, body, 0, unroll=GATHER_UNROLL)


def _row_gather_wait(src_hbm, n_rows, buf, sem):
    n = n_rows * ROW_TILES
    pltpu.make_async_copy(src_hbm.at[pl.ds(0, n)], buf.at[pl.ds(0, n)], sem).wait()


def _moe_kernel(te_ref, tf_ref, tn_ref, ts_ref, nu_ref, src_ref, x_hbm, gffn_ref, wg_hbm, wu_hbm, wd_hbm,
                y_ref, xbuf, sem, wgf, wuf, wdf, wsem, wgb, wub, wdb):
    i = pl.program_id(0)
    tm = y_ref.shape[0] // ROW_PITCH
    n_used = nu_ref[0]
    slot = lax.rem(i, 2)

    def weight_copies(e, s):
        return (pltpu.make_async_copy(wg_hbm.at[e], wgf.at[s], wsem.at[s]),
                pltpu.make_async_copy(wu_hbm.at[e], wuf.at[s], wsem.at[s]),
                pltpu.make_async_copy(wd_hbm.at[e], wdf.at[s], wsem.at[s]))

    @pl.when(i == 0)
    def _():
        _row_gather_start(x_hbm, src_ref, 0, tm, xbuf.at[0], sem.at[0])
        for cp in weight_copies(te_ref[0], 0):
            cp.start(priority=WEIGHT_DMA_PRIORITY)

    @pl.when(i + 1 < n_used)
    def _():
        _row_gather_start(x_hbm, src_ref, (i + 1) * tm, tm, xbuf.at[1 - slot], sem.at[1 - slot])

    @pl.when(i < n_used)
    def _():
        @pl.when(tf_ref[i] == 1)
        def _():
            ws = ts_ref[i]
            for cp in weight_copies(te_ref[i], ws):
                cp.wait()
            for s in range(2):
                @pl.when(ws == s)
                def _(s=s):
                    wgb[...] = wgf[s].astype(BF16)
                    wub[...] = wuf[s].astype(BF16)
                    wdb[...] = wdf[s].astype(BF16)

            @pl.when(tn_ref[i] >= 0)
            def _():
                for cp in weight_copies(tn_ref[i], 1 - ws):
                    cp.start(priority=WEIGHT_DMA_PRIORITY)

        _row_gather_wait(x_hbm, tm, xbuf.at[slot], sem.at[slot])
        xs = _load_token_rows(xbuf.at[slot], tm)
        ms = jnp.mean(xs * xs, axis=-1, keepdims=True)
        hn = (xs * lax.rsqrt(ms + EPS) * gffn_ref[...]).astype(BF16)
        gate = jnp.dot(hn, wgb[...], preferred_element_type=F32)
        up = jnp.dot(hn, wub[...], preferred_element_type=F32)
        h = jax.nn.silu(gate) * up
        _store_token_rows(y_ref, jnp.dot(h.astype(BF16), wdb[...], preferred_element_type=F32))

    @pl.when(i >= n_used)
    def _():
        y_ref[...] = jnp.zeros(y_ref.shape, y_ref.dtype)


def _moe_ffn(xnew, tile_e, tile_first, tile_next, tile_wslot, n_used, src_row, gffn, w_gate, w_up, w_down,
             *, tm):
    p_rows = src_row.shape[0]
    n_tiles = p_rows // tm
    grid_spec = pltpu.PrefetchScalarGridSpec(
        num_scalar_prefetch=6,
        grid=(n_tiles,),
        in_specs=[
            pl.BlockSpec(memory_space=pl.ANY),
            pl.BlockSpec((1, D_MODEL), lambda i, *_: (0, 0)),
            pl.BlockSpec(memory_space=pl.ANY),
            pl.BlockSpec(memory_space=pl.ANY),
            pl.BlockSpec(memory_space=pl.ANY),
        ],
        out_specs=pl.BlockSpec((tm * ROW_PITCH, LANES), lambda i, *_: (i, 0)),
        scratch_shapes=[
            pltpu.VMEM((2, tm * ROW_PITCH, LANES), F32),
            pltpu.SemaphoreType.DMA((2,)),
            pltpu.VMEM((2, D_MODEL, D_EXPERT), F32),
            pltpu.VMEM((2, D_MODEL, D_EXPERT), F32),
            pltpu.VMEM((2, D_EXPERT, D_MODEL), F32),
            pltpu.SemaphoreType.DMA((2,)),
            pltpu.VMEM((D_MODEL, D_EXPERT), BF16),
            pltpu.VMEM((D_MODEL, D_EXPERT), BF16),
            pltpu.VMEM((D_EXPERT, D_MODEL), BF16),
        ],
    )
    return pl.pallas_call(
        _moe_kernel,
        grid_spec=grid_spec,
        out_shape=jax.ShapeDtypeStruct((p_rows * ROW_PITCH, LANES), F32),
        compiler_params=_cparams(("arbitrary",)),
        name="moe_ffn",
    )(tile_e, tile_first, tile_next, tile_wslot, n_used, src_row, xnew, gffn, w_gate, w_up, w_down)


def _combine_kernel(pos_ref, xnew_ref, wts_ref, y_hbm, out0_ref, out1_ref, ybuf, sem, *, nt0, n_tok):
    i = pl.program_id(0)
    nt = pl.num_programs(0)
    tm = xnew_ref.shape[0] // ROW_PITCH
    slot = lax.rem(i, 2)

    def start(tile, s):
        _row_gather_start(y_hbm, pos_ref, tile * tm, tm, ybuf.at[s, 0], sem.at[s])
        _row_gather_start(y_hbm, pos_ref, n_tok + tile * tm, tm, ybuf.at[s, 1], sem.at[s], priority=1)

    @pl.when(i == 0)
    def _():
        start(0, 0)

    @pl.when(i + 1 < nt)
    def _():
        start(i + 1, 1 - slot)

    _row_gather_wait(y_hbm, tm, ybuf.at[slot, 0], sem.at[slot])
    _row_gather_wait(y_hbm, tm, ybuf.at[slot, 1], sem.at[slot])
    wt = wts_ref[...].T
    w1 = jnp.broadcast_to(wt[:, 0:1], (tm, LANES))
    w2 = jnp.broadcast_to(wt[:, 1:2], (tm, LANES))

    def write(out_ref):
        for s in range(ROW_TILES):
            out_ref[:, s * LANES:(s + 1) * LANES] = (
                _token_slab(xnew_ref, tm, s)
                + (w1 * _token_slab(ybuf.at[slot, 0], tm, s) + w2 * _token_slab(ybuf.at[slot, 1], tm, s)))

    @pl.when(i < nt0)
    def _():
        write(out0_ref)

    @pl.when(i >= nt0)
    def _():
        write(out1_ref)


def _combine(xnew, y_sorted, wts, pos, t0, *, tm):
    t = xnew.shape[0] // ROW_PITCH
    nt0 = t0 // tm
    nt = t // tm
    grid_spec = pltpu.PrefetchScalarGridSpec(
        num_scalar_prefetch=1,
        grid=(nt,),
        in_specs=[
            pl.BlockSpec((tm * ROW_PITCH, LANES), lambda i, ps: (i, 0)),
            pl.BlockSpec((8, tm), lambda i, ps: (0, i)),
            pl.BlockSpec(memory_space=pl.ANY),
        ],
        out_specs=[
            pl.BlockSpec((tm, D_MODEL), lambda i, ps: (jnp.minimum(i, nt0 - 1), 0)),
            pl.BlockSpec((tm, D_MODEL), lambda i, ps: (jnp.maximum(i - nt0, 0), 0)),
        ],
        scratch_shapes=[
            pltpu.VMEM((2, 2, tm * ROW_PITCH, LANES), F32),
            pltpu.SemaphoreType.DMA((2,)),
        ],
    )
    return pl.pallas_call(
        functools.partial(_combine_kernel, nt0=nt0, n_tok=t),
        grid_spec=grid_spec,
        out_shape=[
            jax.ShapeDtypeStruct((t0, D_MODEL), F32),
            jax.ShapeDtypeStruct((t - t0, D_MODEL), F32),
        ],
        compiler_params=_cparams(("arbitrary",)),
        name="moe_combine",
    )(pos, xnew, wts, y_sorted)


def _rope_tables(max_len):
    inv = 1.0 / (ROPE_THETA ** (jnp.arange(0, HEAD_DIM, 2, dtype=F32) / HEAD_DIM))
    ang = jnp.arange(max_len, dtype=F32)[:, None] * inv[None, :]
    ang = jnp.concatenate([ang, ang, ang, ang], axis=-1)
    lane = jnp.arange(LANES)
    sign = jnp.where((lane % HEAD_DIM) < HEAD_DIM // 2, -1.0, 1.0).astype(F32)
    return jnp.cos(ang), jnp.sin(ang) * sign[None, :]


def _tile_tables(groups, tm):
    pos, length = [], []
    for batch, s_len in groups:
        for _ in range(batch):
            for p in range(0, s_len, tm):
                pos.append(p)
                length.append(s_len)
    return np.asarray(pos, np.int32), np.asarray(length, np.int32)


def _routing_tables(ids, n_tok, tm):
    e_flat = ids[0:2, :].reshape(-1)
    onehot = (e_flat[:, None] == jnp.arange(N_EXPERTS, dtype=I32)[None, :]).astype(I32)
    csum = jnp.cumsum(onehot, axis=0)
    rank = jnp.sum(csum * onehot, axis=1) - 1
    counts = csum[-1]
    padded = ((counts + tm - 1) // tm) * tm
    ends = jnp.cumsum(padded)
    starts = ends - padded
    dst = (starts[e_flat] + rank).astype(I32)
    p_rows = 2 * n_tok + N_EXPERTS * tm
    n_tiles = p_rows // tm
    tok = jnp.tile(jnp.arange(n_tok, dtype=I32), 2)
    src_row = jnp.zeros((p_rows,), I32).at[dst].set(tok)
    n_used = (ends[-1] // tm).astype(I32)
    tile_start = jnp.arange(n_tiles, dtype=I32) * tm
    tile_e = jnp.sum((ends[None, :] <= tile_start[:, None]).astype(I32), axis=1)
    tile_e = jnp.minimum(tile_e, N_EXPERTS - 1)
    active = jnp.arange(n_tiles, dtype=I32) < n_used
    last_e = tile_e[jnp.maximum(n_used - 1, 0)]
    tile_e = jnp.where(active, tile_e, last_e)
    tile_first = (active & (tile_start == starts[tile_e])).astype(I32)
    tile_wslot = lax.rem(jnp.cumsum(tile_first) - 1, 2).astype(I32)
    e_ids = jnp.arange(N_EXPERTS, dtype=I32)
    later_used = (counts[None, :] > 0) & (e_ids[None, :] > e_ids[:, None])
    next_used = jnp.min(jnp.where(later_used, e_ids[None, :], N_EXPERTS), axis=1)
    next_used = jnp.where(next_used == N_EXPERTS, -1, next_used).astype(I32)
    tile_next = next_used[tile_e]
    return tile_e, tile_first, tile_next, tile_wslot, n_used.reshape(1), src_row, dst


def _forward(x0, x1, groups, params, *, tm_proj, tn_proj, rc_proj, tq, tk, tm_post, tm_moe, tm_comb):
    (attn_norm_g, w_in, q_norm_g, k_norm_g, lambda_q1, lambda_k1, lambda_q2, lambda_k2, subln_g,
     w_attn_up, w_pool_grp, pool_scale, w_pool_up, w_out, ffn_norm_g, w_group_router,
     b_group_router, w_expert_router, b_expert_router, w_gate, w_up, w_down) = params
    t0, t1 = x0.shape[0], x1.shape[0]
    n_tok = t0 + t1
    lam_init = 0.8 - 0.6 * math.exp(-0.3 * 0)
    max_len = max(s for _, s in groups)

    cos_t, sin_t = _rope_tables(max_len)
    pos_proj, _ = _tile_tables(groups, tm_proj)
    q_scale = HEAD_DIM ** -0.5 * math.log2(math.e)
    qkg = jnp.zeros((8, ATTN_W), F32)
    qkg = qkg.at[0].set(jnp.tile(q_norm_g, ATTN_W // HEAD_DIM) * q_scale)
    qkg = qkg.at[1].set(jnp.tile(k_norm_g, ATTN_W // HEAD_DIM))
    seg = np.arange(LANES) // HEAD_DIM
    bd = jnp.asarray((seg[:, None] == seg[None, :]).astype(np.float32) / HEAD_DIM, BF16)

    proj = _in_projection(x0, x1, jnp.asarray(pos_proj // tm_proj), attn_norm_g[None, :],
                          w_in.astype(BF16), cos_t, sin_t, qkg, bd, tm=tm_proj, tn=tn_proj, rc=rc_proj)

    lam_pack = jnp.zeros((8, HEAD_DIM), F32)
    lam_pack = lam_pack.at[0].set(lambda_q1).at[1].set(lambda_k1).at[2].set(lambda_q2).at[3].set(lambda_k2)
    outs = []
    tok_off = 0
    for batch, s_len in groups:
        outs.append(_attention(proj, tok_off, batch, s_len, lam_pack, subln_g[None, :],
                               tq=min(tq, s_len), tk=min(tk, s_len), lam_init=lam_init))
        tok_off += batch * s_len

    tile_pos, tile_len = _tile_tables(groups, tm_post)
    w_router = jnp.zeros((D_MODEL, ROUTER_COLS), F32)
    w_router = w_router.at[:, 0:N_GROUPS].set(w_group_router)
    w_router = w_router.at[:, EXPERT_COL0:EXPERT_COL0 + N_EXPERTS].set(w_expert_router)
    b_router = jnp.zeros((1, ROUTER_COLS), F32)
    b_router = b_router.at[0, 0:N_GROUPS].set(b_group_router)
    b_router = b_router.at[0, EXPERT_COL0:EXPERT_COL0 + N_EXPERTS].set(b_expert_router)
    wr_hi = w_router.astype(BF16)
    wr_lo = (w_router - wr_hi.astype(F32)).astype(BF16)
    xnew, ids, wts = _post_block(
        x0, x1, outs[0], outs[1], proj, jnp.asarray(tile_pos), jnp.asarray(tile_len),
        w_attn_up.astype(BF16), w_pool_grp.astype(BF16), pool_scale[None, :], w_pool_up.astype(BF16),
        w_out.astype(BF16), ffn_norm_g[None, :], jnp.concatenate([wr_hi, wr_lo], axis=1), b_router,
        tm=tm_post)

    tile_e, tile_first, tile_next, tile_wslot, n_used, src_row, dst = _routing_tables(ids, n_tok, tm_moe)
    y_sorted = _moe_ffn(
        xnew, tile_e, tile_first, tile_next, tile_wslot, n_used, src_row, ffn_norm_g[None, :],
        w_gate.reshape(N_EXPERTS, D_MODEL, D_EXPERT), w_up.reshape(N_EXPERTS, D_MODEL, D_EXPERT),
        w_down.reshape(N_EXPERTS, D_EXPERT, D_MODEL), tm=tm_moe)
    return _combine(xnew, y_sorted, wts, dst, t0, tm=tm_comb)


def kernel(x_prompt, x_sample, attn_norm_g, w_in, q_norm_g, k_norm_g, lambda_q1, lambda_k1, lambda_q2,
           lambda_k2, subln_g, w_attn_up, w_pool_grp, pool_scale, w_pool_up, w_out, ffn_norm_g,
           w_group_router, b_group_router, w_expert_router, b_expert_router, w_gate, w_up, w_down):
    assert attn_norm_g.shape[0] == 1, "single-layer stack"
    params = tuple(p[0] for p in (
        attn_norm_g, w_in, q_norm_g, k_norm_g, lambda_q1, lambda_k1, lambda_q2, lambda_k2, subln_g,
        w_attn_up, w_pool_grp, pool_scale, w_pool_up, w_out, ffn_norm_g, w_group_router,
        b_group_router, w_expert_router, b_expert_router, w_gate, w_up, w_down))
    groups = (x_prompt.shape[:2], x_sample.shape[:2])
    y0, y1 = _forward(
        x_prompt.reshape(-1, D_MODEL), x_sample.reshape(-1, D_MODEL), groups, params,
        tm_proj=1024, tn_proj=1024, rc_proj=512, tq=2048, tk=512, tm_post=256, tm_moe=256, tm_comb=256)
    return y0.reshape(x_prompt.shape), y1.reshape(x_sample.shape)
```

```python
import functools
import math

import numpy as np
import jax
import jax.numpy as jnp
from jax import lax
from jax.experimental import pallas as pl
from jax.experimental.pallas import tpu as pltpu

F32 = jnp.float32
BF16 = jnp.bfloat16
I32 = jnp.int32

D_MODEL = 2048
HEAD_DIM = 64
N_HEADS = 8
HEAD_W = 2 * HEAD_DIM
ATTN_W = N_HEADS * HEAD_W
POOL_W = 1024
POOL_WINDOWS = (2, 4, 8, 16)
POOL_GROUP_W = POOL_W // len(POOL_WINDOWS)
POOL_HALO = 64
IN_COLS = 3 * ATTN_W + POOL_W + 2 * D_MODEL
N_GROUPS = 4
EXPERTS_PER_GROUP = 8
N_EXPERTS = N_GROUPS * EXPERTS_PER_GROUP
D_EXPERT = 512
ROPE_THETA = 10000.0
EPS = 1e-6
LANES = 128
ROUTER_COLS = 128
EXPERT_COL0 = 8
ROW_TILES = D_MODEL // LANES
ROW_PITCH = ROW_TILES + 1
SAFE_SCORE_BOUND = 50.0
SCORE_BOUND_MARGIN = 1.02

COL_Q, COL_K, COL_V, COL_PIN, COL_GA, COL_GP = 0, 1, 2, 3, 4, 6

VMEM_LIMIT = 56 * 1024 * 1024


def _cparams(sem, vmem=VMEM_LIMIT):
    return pltpu.CompilerParams(dimension_semantics=sem, vmem_limit_bytes=vmem)


def _token_slab(ref, n_tok, s):
    return ref[pl.ds(s, n_tok, stride=ROW_PITCH), :]


def _load_token_rows(ref, n_tok):
    return jnp.concatenate([_token_slab(ref, n_tok, s) for s in range(ROW_TILES)], axis=1)


def _store_token_rows(ref, value):
    n_tok = value.shape[0]
    for s in range(ROW_TILES):
        ref[pl.ds(s, n_tok, stride=ROW_PITCH), :] = value[:, s * LANES:(s + 1) * LANES]
    ref[pl.ds(ROW_TILES, n_tok, stride=ROW_PITCH), :] = jnp.zeros((n_tok, LANES), ref.dtype)


def _inproj_kernel(posblk_ref, x0_hbm, x1_hbm, gattn_ref, w_ref, cos_ref, sin_ref, qkg_ref, bd_ref,
                   o_ref, xn_ref, xbuf, xsem, *, nt0, rc):
    del posblk_ref
    i = pl.program_id(0)
    j = pl.program_id(1)
    nt = pl.num_programs(0)
    tm = xn_ref.shape[0]
    n_chunks = tm // rc

    def x_tile_copy(x_hbm, tile):
        return pltpu.make_async_copy(x_hbm.at[pl.ds(tile * tm, tm)], xbuf, xsem)

    def start_x_tile(tile):
        @pl.when(tile < nt0)
        def _():
            x_tile_copy(x0_hbm, tile).start()

        @pl.when(tile >= nt0)
        def _():
            x_tile_copy(x1_hbm, tile - nt0).start()

    @pl.when(j == 0)
    def _():
        @pl.when(i == 0)
        def _():
            start_x_tile(i)

        x_tile_copy(x0_hbm, 0).wait()
        for c in range(n_chunks):
            xs = xbuf[c * rc:(c + 1) * rc, :]
            ms = jnp.mean(xs * xs, axis=-1, keepdims=True)
            xn_ref[c * rc:(c + 1) * rc, :] = (xs * lax.rsqrt(ms + EPS) * gattn_ref[...]).astype(BF16)

        @pl.when(i + 1 < nt)
        def _():
            start_x_tile(i + 1)

    def chunk_acc(c):
        return jnp.dot(xn_ref[c * rc:(c + 1) * rc, :], w_ref[...], preferred_element_type=F32)

    @pl.when(j <= COL_K)
    def _():
        g = jnp.where(j == COL_Q, qkg_ref[0:1, :], qkg_ref[1:2, :])
        lane = lax.broadcasted_iota(I32, (rc, LANES), 1)
        first_half = (lane % HEAD_DIM) < (HEAD_DIM // 2)
        for c in range(n_chunks):
            a = chunk_acc(c)
            cos = cos_ref[c * rc:(c + 1) * rc, :]
            sin = sin_ref[c * rc:(c + 1) * rc, :]
            for t in range(a.shape[1] // LANES):
                at = a[:, t * LANES:(t + 1) * LANES]
                ms = jnp.dot((at * at).astype(BF16), bd_ref[...], preferred_element_type=F32)
                u = at * g[:, t * LANES:(t + 1) * LANES]
                rot = jnp.where(first_half, pltpu.roll(u, LANES - HEAD_DIM // 2, 1),
                                pltpu.roll(u, HEAD_DIM // 2, 1))
                val = (u * cos + rot * sin) * lax.rsqrt(ms + EPS)
                o_ref[c * rc:(c + 1) * rc, t * LANES:(t + 1) * LANES] = val.astype(BF16)

    @pl.when((j == COL_V) | (j == COL_PIN))
    def _():
        for c in range(n_chunks):
            o_ref[c * rc:(c + 1) * rc, :] = chunk_acc(c).astype(BF16)

    @pl.when(j >= COL_GA)
    def _():
        for c in range(n_chunks):
            o_ref[c * rc:(c + 1) * rc, :] = jax.nn.sigmoid(chunk_acc(c)).astype(BF16)


def _in_projection(x0, x1, posblk, gattn, w_in, cos_t, sin_t, qkg, bd, *, tm, tn, rc):
    t0, t1 = x0.shape[0], x1.shape[0]
    nt0, nt1 = t0 // tm, t1 // tm
    nt = nt0 + nt1
    grid_spec = pltpu.PrefetchScalarGridSpec(
        num_scalar_prefetch=1,
        grid=(nt, IN_COLS // tn),
        in_specs=[
            pl.BlockSpec(memory_space=pl.ANY),
            pl.BlockSpec(memory_space=pl.ANY),
            pl.BlockSpec((1, D_MODEL), lambda i, j, pb: (0, 0)),
            pl.BlockSpec((D_MODEL, tn), lambda i, j, pb: (0, j)),
            pl.BlockSpec((tm, LANES), lambda i, j, pb: (pb[i], 0)),
            pl.BlockSpec((tm, LANES), lambda i, j, pb: (pb[i], 0)),
            pl.BlockSpec((8, tn), lambda i, j, pb: (0, 0)),
            pl.BlockSpec((LANES, LANES), lambda i, j, pb: (0, 0)),
        ],
        out_specs=pl.BlockSpec((tm, tn), lambda i, j, pb: (i, j)),
        scratch_shapes=[
            pltpu.VMEM((tm, D_MODEL), BF16),
            pltpu.VMEM((tm, D_MODEL), F32),
            pltpu.SemaphoreType.DMA(()),
        ],
    )
    return pl.pallas_call(
        functools.partial(_inproj_kernel, nt0=nt0, rc=rc),
        grid_spec=grid_spec,
        out_shape=jax.ShapeDtypeStruct((t0 + t1, IN_COLS), BF16),
        compiler_params=_cparams(("arbitrary", "arbitrary")),
        name="in_projection",
    )(posblk, x0, x1, gattn, w_in, cos_t, sin_t, qkg, bd)


def _attn_kernel(q_ref, k_ref, v_ref, lam_ref, sg_ref, o_ref, vt_ref, acc_ref, m_ref, l_ref, kn_ref,
                 p_ref, *, tk, lam_init):
    i = pl.program_id(2)
    s_len = k_ref.shape[0]
    tq = q_ref.shape[0]
    nk = s_len // tk
    nt_dims = (((1,), (1,)), ((), ()))

    sel_r = lax.broadcasted_iota(I32, (8, HEAD_W), 0)
    sel_l = lax.broadcasted_iota(I32, (8, HEAD_W), 1)
    sel = jnp.where(sel_l // HEAD_DIM == sel_r, 1.0, 0.0).astype(BF16)

    @pl.when(i == 0)
    def _():
        kn2 = jnp.zeros((8, 1), F32)
        for jb in range(nk):
            rows = slice(jb * tk, (jb + 1) * tk)
            vt_ref[:, rows] = v_ref[rows, :].astype(F32).T.astype(BF16)
            kf = k_ref[rows, :].astype(F32)
            n2 = lax.dot_general(sel, (kf * kf).astype(BF16), nt_dims, preferred_element_type=F32)
            kn2 = jnp.maximum(kn2, jnp.max(n2, axis=1, keepdims=True))
        kn_ref[...] = jnp.broadcast_to(kn2, kn_ref.shape)

    q = q_ref[...]
    lane = lax.broadcasted_iota(I32, q.shape, 1)
    zero = jnp.zeros_like(q)
    qc = (jnp.where(lane < HEAD_DIM, q, zero), jnp.where(lane >= HEAD_DIM, q, zero))

    qf = q.astype(F32)
    qn2 = lax.dot_general(sel, (qf * qf).astype(BF16), nt_dims, preferred_element_type=F32)
    bound = jnp.sqrt(qn2 * kn_ref[:, 0:1]) * SCORE_BOUND_MARGIN
    bound_c = (bound[0:1, :], bound[1:2, :])
    use_bound = jnp.max(bound[0:2, :]) <= SAFE_SCORE_BOUND

    l_ref[...] = jnp.zeros(l_ref.shape, F32)
    acc_ref[...] = jnp.zeros(acc_ref.shape, F32)

    def load_blocks(jb):
        off = jb * tk if isinstance(jb, int) else pl.multiple_of(jb * tk, tk)
        return k_ref[pl.ds(off, tk), :], vt_ref[:, pl.ds(off, tk)]

    def bounded_weights(jb, slot):
        kb, _ = load_blocks(jb)
        for c in range(2):
            st = lax.dot_general(kb, qc[c], nt_dims, preferred_element_type=F32)
            p = jnp.exp2(st - bound_c[c])
            l_ref[c] += jnp.sum(p, axis=0, keepdims=True)
            p_ref[slot, c] = p.astype(BF16)

    def weighted_values(jb, slot):
        _, vtb = load_blocks(jb)
        for c in range(2):
            acc_ref[c] += jnp.dot(vtb, p_ref[slot, c], preferred_element_type=F32)

    def bounded_pair(m, carry):
        j0 = 2 * m
        bounded_weights(j0 + 1, 1)
        weighted_values(j0, 0)
        bounded_weights(j0 + 2, 0)
        weighted_values(j0 + 1, 1)
        return carry

    def running_max_step(jb, carry):
        kb, vtb = load_blocks(jb)
        for c in range(2):
            st = lax.dot_general(kb, qc[c], nt_dims, preferred_element_type=F32)
            m_old = m_ref[c]
            m_new = jnp.maximum(m_old, jnp.max(st, axis=0, keepdims=True))
            alpha = jnp.exp2(m_old - m_new)
            p = jnp.exp2(st - m_new)
            l_ref[c] = alpha * l_ref[c] + jnp.sum(p, axis=0, keepdims=True)
            acc_ref[c] = alpha * acc_ref[c] + jnp.dot(vtb, p.astype(BF16), preferred_element_type=F32)
            m_ref[c] = m_new
        return carry

    @pl.when(use_bound)
    def _():
        bounded_weights(0, 0)
        lax.fori_loop(0, nk // 2 - 1, bounded_pair, 0)
        bounded_weights(nk - 1, 1)
        weighted_values(nk - 2, 0)
        weighted_values(nk - 1, 1)

    @pl.when(jnp.logical_not(use_bound))
    def _():
        m_ref[...] = jnp.full(m_ref.shape, -jnp.inf, F32)
        lax.fori_loop(0, nk, running_max_step, 0)

    lq1, lk1, lq2, lk2 = lam_ref[0:1, :], lam_ref[1:2, :], lam_ref[2:3, :], lam_ref[3:4, :]
    lam = (jnp.exp(jnp.sum(lq1 * lk1, axis=-1, keepdims=True))
           - jnp.exp(jnp.sum(lq2 * lk2, axis=-1, keepdims=True)) + lam_init)
    ot = acc_ref[0] / l_ref[0] - lam * (acc_ref[1] / l_ref[1])
    ms = jnp.mean(ot * ot, axis=0, keepdims=True)
    on = (ot * lax.rsqrt(ms + EPS)).T
    o_ref[...] = (on * sg_ref[...] * (1.0 - lam_init)).astype(BF16)


def _attention(proj, tok_off, batch, s_len, lam_pack, subln_g, *, tq, tk, lam_init):
    qb0 = tok_off // tq
    sb0 = tok_off // s_len
    nq = s_len // tq
    return pl.pallas_call(
        functools.partial(_attn_kernel, tk=tk, lam_init=lam_init),
        grid=(batch, N_HEADS, nq),
        in_specs=[
            pl.BlockSpec((tq, HEAD_W), lambda b, h, i: (qb0 + b * nq + i, COL_Q * N_HEADS + h)),
            pl.BlockSpec((s_len, HEAD_W), lambda b, h, i: (sb0 + b, COL_K * N_HEADS + h)),
            pl.BlockSpec((s_len, HEAD_W), lambda b, h, i: (sb0 + b, COL_V * N_HEADS + h)),
            pl.BlockSpec((8, HEAD_DIM), lambda b, h, i: (0, 0)),
            pl.BlockSpec((1, HEAD_W), lambda b, h, i: (0, 0)),
        ],
        out_specs=pl.BlockSpec((tq, HEAD_W), lambda b, h, i: (b * nq + i, h)),
        out_shape=jax.ShapeDtypeStruct((batch * s_len, ATTN_W), BF16),
        scratch_shapes=[
            pltpu.VMEM((HEAD_W, s_len), BF16),
            pltpu.VMEM((2, HEAD_W, tq), F32),
            pltpu.VMEM((2, 1, tq), F32),
            pltpu.VMEM((2, 1, tq), F32),
            pltpu.VMEM((8, LANES), F32),
            pltpu.VMEM((2, 2, tk, tq), BF16),
        ],
        compiler_params=_cparams(("arbitrary", "arbitrary", "arbitrary")),
        name="diff_attention",
    )(proj, proj, proj, lam_pack, subln_g)


def _post_kernel(pos_ref, len_ref, x0_ref, x1_ref, o0_ref, o1_ref, pin_ref, prev_ref, next_ref,
                 ga_ref, gp_ref, wau_ref, wgrp_ref, pscale_ref, wpu_ref, wout_ref, gffn_ref,
                 wrc_ref, br_ref, xnew_ref, ids_ref, wts_ref, *, nt0):
    i = pl.program_id(0)
    tm = pin_ref.shape[0]
    p0 = pos_ref[i]
    s_len = len_ref[i]
    first = i < nt0

    o = jnp.where(first, o0_ref[...], o1_ref[...])
    attn_d = jnp.dot(o, wau_ref[...], preferred_element_type=F32)

    ext = jnp.concatenate([prev_ref[...], pin_ref[...], next_ref[...]], axis=0)
    ke = tm + 2 * POOL_HALO
    r_band = lax.broadcasted_iota(I32, (tm, ke), 0) + p0
    c_band = lax.broadcasted_iota(I32, (tm, ke), 1) + (p0 - POOL_HALO)
    r_cnt = lax.broadcasted_iota(I32, (tm, POOL_GROUP_W), 0) + p0
    pooled = []
    for gi, w in enumerate(POOL_WINDOWS):
        back, fwd = w // 2, w - 1 - w // 2
        lo = jnp.maximum(r_band - back, 0)
        hi = jnp.minimum(r_band + fwd, s_len - 1)
        band = jnp.where((c_band >= lo) & (c_band <= hi), 1.0, 0.0).astype(BF16)
        cnt = jnp.minimum(r_cnt + fwd, s_len - 1) - jnp.maximum(r_cnt - back, 0) + 1
        cols = slice(gi * POOL_GROUP_W, (gi + 1) * POOL_GROUP_W)
        wsum = jnp.dot(band, ext[:, cols], preferred_element_type=F32)
        y = wsum / cnt.astype(F32) - pin_ref[:, cols].astype(F32)
        pooled.append(jnp.dot(y.astype(BF16), wgrp_ref[gi], preferred_element_type=F32))
    pooled = (jnp.concatenate(pooled, axis=1) * pscale_ref[...]).astype(BF16)
    pool_d = jnp.dot(pooled, wpu_ref[...], preferred_element_type=F32)

    merged = ga_ref[...].astype(F32) * attn_d + gp_ref[...].astype(F32) * pool_d
    x = jnp.where(first, x0_ref[...], x1_ref[...])
    xnew = x + jnp.dot(merged.astype(BF16), wout_ref[...], preferred_element_type=F32)
    _store_token_rows(xnew_ref, xnew)

    ms = jnp.mean(xnew * xnew, axis=-1, keepdims=True)
    hn = xnew * lax.rsqrt(ms + EPS) * gffn_ref[...]
    hi_part = hn.astype(BF16)
    lo_part = (hn - hi_part.astype(F32)).astype(BF16)
    hw = jnp.dot(hi_part, wrc_ref[...], preferred_element_type=F32)
    logits = (hw[:, 0:ROUTER_COLS]
              + (hw[:, ROUTER_COLS:]
                 + jnp.dot(lo_part, wrc_ref[:, 0:ROUTER_COLS], preferred_element_type=F32))
              + br_ref[...])
    lt = logits.T

    gl = lt[0:N_GROUPS, :]
    g_iota = lax.broadcasted_iota(I32, gl.shape, 0)
    g_max = jnp.max(gl, axis=0, keepdims=True)
    g_sel = jnp.min(jnp.where(gl == g_max, g_iota, N_GROUPS), axis=0, keepdims=True)
    g_w = 1.0 / jnp.sum(jnp.exp(gl - g_max), axis=0, keepdims=True)
    el = lt[EXPERT_COL0:EXPERT_COL0 + N_EXPERTS, :]
    e_iota = lax.broadcasted_iota(I32, el.shape, 0)
    neg = jnp.float32(-jnp.inf)
    el = jnp.where(e_iota // EXPERTS_PER_GROUP == g_sel, el, neg)
    v1 = jnp.max(el, axis=0, keepdims=True)
    i1 = jnp.min(jnp.where(el == v1, e_iota, N_EXPERTS), axis=0, keepdims=True)
    el2 = jnp.where(e_iota == i1, neg, el)
    v2 = jnp.max(el2, axis=0, keepdims=True)
    i2 = jnp.min(jnp.where(el2 == v2, e_iota, N_EXPERTS), axis=0, keepdims=True)
    d = jnp.exp(v2 - v1)
    w1 = g_w / (1.0 + d)
    w2 = g_w * d / (1.0 + d)
    row = lax.broadcasted_iota(I32, (8, tm), 0)
    ids_ref[...] = jnp.where(row == 0, i1, jnp.where(row == 1, i2, 0))
    wts_ref[...] = jnp.where(row == 0, w1, jnp.where(row == 1, w2, 0.0))


def _post_block(x0, x1, o0, o1, proj, tile_pos, tile_len, wau, wgrp, pscale, wpu, wout, gffn,
                wrc, br, *, tm):
    t0, t1 = x0.shape[0], x1.shape[0]
    t = t0 + t1
    nt0 = t0 // tm
    nt = t // tm
    hb = tm // POOL_HALO
    n_hblk = t // POOL_HALO
    const = lambda *shape: pl.BlockSpec(shape, lambda i, tp, tl: (0,) * len(shape),
                                        pipeline_mode=pl.Buffered(1))
    grid_spec = pltpu.PrefetchScalarGridSpec(
        num_scalar_prefetch=2,
        grid=(nt,),
        in_specs=[
            pl.BlockSpec((tm, D_MODEL), lambda i, tp, tl: (jnp.minimum(i, nt0 - 1), 0)),
            pl.BlockSpec((tm, D_MODEL), lambda i, tp, tl: (jnp.maximum(i - nt0, 0), 0)),
            pl.BlockSpec((tm, ATTN_W), lambda i, tp, tl: (jnp.minimum(i, nt0 - 1), 0)),
            pl.BlockSpec((tm, ATTN_W), lambda i, tp, tl: (jnp.maximum(i - nt0, 0), 0)),
            pl.BlockSpec((tm, POOL_W), lambda i, tp, tl: (i, COL_PIN)),
            pl.BlockSpec((POOL_HALO, POOL_W), lambda i, tp, tl: (jnp.maximum(i * hb - 1, 0), COL_PIN)),
            pl.BlockSpec((POOL_HALO, POOL_W), lambda i, tp, tl: (jnp.minimum((i + 1) * hb, n_hblk - 1), COL_PIN)),
            pl.BlockSpec((tm, D_MODEL), lambda i, tp, tl: (i, COL_GA // 2)),
            pl.BlockSpec((tm, D_MODEL), lambda i, tp, tl: (i, COL_GP // 2)),
            const(ATTN_W, D_MODEL),
            const(len(POOL_WINDOWS), POOL_GROUP_W, POOL_GROUP_W),
            const(1, POOL_W),
            const(POOL_W, D_MODEL),
            const(D_MODEL, D_MODEL),
            const(1, D_MODEL),
            const(D_MODEL, 2 * ROUTER_COLS),
            const(1, ROUTER_COLS),
        ],
        out_specs=[
            pl.BlockSpec((tm * ROW_PITCH, LANES), lambda i, tp, tl: (i, 0)),
            pl.BlockSpec((8, tm), lambda i, tp, tl: (0, i)),
            pl.BlockSpec((8, tm), lambda i, tp, tl: (0, i)),
        ],
    )
    return pl.pallas_call(
        functools.partial(_post_kernel, nt0=nt0),
        grid_spec=grid_spec,
        out_shape=[
            jax.ShapeDtypeStruct((t * ROW_PITCH, LANES), F32),
            jax.ShapeDtypeStruct((8, t), I32),
            jax.ShapeDtypeStruct((8, t), F32),
        ],
        compiler_params=_cparams(("arbitrary",)),
        name="post_block",
    )(tile_pos, tile_len, x0, x1, o0, o1, proj, proj, proj, proj, proj, wau, wgrp, pscale, wpu,
      wout, gffn, wrc, br)


GATHER_UNROLL = 8
WEIGHT_DMA_PRIORITY = 1


def _row_gather_start(src_hbm, idx_ref, base, n_rows, buf, sem, priority=0):
    def body(r, carry):
        src0 = idx_ref[base + r] * ROW_PITCH
        dst0 = r * ROW_PITCH
        pltpu.make_async_copy(src_hbm.at[pl.ds(src0, ROW_TILES)], buf.at[pl.ds(dst0, ROW_TILES)],
                              sem).start(priority=priority)
        return carry
    lax.fori_loop(0, n_rows, body, 0, unroll=GATHER_UNROLL)


def _row_gather_wait(src_hbm, n_rows, buf, sem):
    n = n_rows * ROW_TILES
    pltpu.make_async_copy(src_hbm.at[pl.ds(0, n)], buf.at[pl.ds(0, n)], sem).wait()


def _moe_kernel(te_ref, tf_ref, tn_ref, ts_ref, nu_ref, src_ref, x_hbm, gffn_ref, wg_hbm, wu_hbm, wd_hbm,
                y_ref, xbuf, sem, wgf, wuf, wdf, wsem, wgb, wub, wdb):
    i = pl.program_id(0)
    tm = y_ref.shape[0] // ROW_PITCH
    n_used = nu_ref[0]
    slot = lax.rem(i, 2)

    def weight_copies(e, s):
        return (pltpu.make_async_copy(wg_hbm.at[e], wgf.at[s], wsem.at[s]),
                pltpu.make_async_copy(wu_hbm.at[e], wuf.at[s], wsem.at[s]),
                pltpu.make_async_copy(wd_hbm.at[e], wdf.at[s], wsem.at[s]))

    @pl.when(i == 0)
    def _():
        _row_gather_start(x_hbm, src_ref, 0, tm, xbuf.at[0], sem.at[0])
        for cp in weight_copies(te_ref[0], 0):
            cp.start(priority=WEIGHT_DMA_PRIORITY)

    @pl.when(i + 1 < n_used)
    def _():
        _row_gather_start(x_hbm, src_ref, (i + 1) * tm, tm, xbuf.at[1 - slot], sem.at[1 - slot])

    @pl.when(i < n_used)
    def _():
        @pl.when(tf_ref[i] == 1)
        def _():
            ws = ts_ref[i]
            for cp in weight_copies(te_ref[i], ws):
                cp.wait()
            for s in range(2):
                @pl.when(ws == s)
                def _(s=s):
                    wgb[...] = wgf[s].astype(BF16)
                    wub[...] = wuf[s].astype(BF16)
                    wdb[...] = wdf[s].astype(BF16)

            @pl.when(tn_ref[i] >= 0)
            def _():
                for cp in weight_copies(tn_ref[i], 1 - ws):
                    cp.start(priority=WEIGHT_DMA_PRIORITY)

        _row_gather_wait(x_hbm, tm, xbuf.at[slot], sem.at[slot])
        xs = _load_token_rows(xbuf.at[slot], tm)
        ms = jnp.mean(xs * xs, axis=-1, keepdims=True)
        hn = (xs * lax.rsqrt(ms + EPS) * gffn_ref[...]).astype(BF16)
        gate = jnp.dot(hn, wgb[...], preferred_element_type=F32)
        up = jnp.dot(hn, wub[...], preferred_element_type=F32)
        h = jax.nn.silu(gate) * up
        _store_token_rows(y_ref, jnp.dot(h.astype(BF16), wdb[...], preferred_element_type=F32))

    @pl.when(i >= n_used)
    def _():
        y_ref[...] = jnp.zeros(y_ref.shape, y_ref.dtype)


def _moe_ffn(xnew, tile_e, tile_first, tile_next, tile_wslot, n_used, src_row, gffn, w_gate, w_up, w_down,
             *, tm):
    p_rows = src_row.shape[0]
    n_tiles = p_rows // tm
    grid_spec = pltpu.PrefetchScalarGridSpec(
        num_scalar_prefetch=6,
        grid=(n_tiles,),
        in_specs=[
            pl.BlockSpec(memory_space=pl.ANY),
            pl.BlockSpec((1, D_MODEL), lambda i, *_: (0, 0)),
            pl.BlockSpec(memory_space=pl.ANY),
            pl.BlockSpec(memory_space=pl.ANY),
            pl.BlockSpec(memory_space=pl.ANY),
        ],
        out_specs=pl.BlockSpec((tm * ROW_PITCH, LANES), lambda i, *_: (i, 0)),
        scratch_shapes=[
            pltpu.VMEM((2, tm * ROW_PITCH, LANES), F32),
            pltpu.SemaphoreType.DMA((2,)),
            pltpu.VMEM((2, D_MODEL, D_EXPERT), F32),
            pltpu.VMEM((2, D_MODEL, D_EXPERT), F32),
            pltpu.VMEM((2, D_EXPERT, D_MODEL), F32),
            pltpu.SemaphoreType.DMA((2,)),
            pltpu.VMEM((D_MODEL, D_EXPERT), BF16),
            pltpu.VMEM((D_MODEL, D_EXPERT), BF16),
            pltpu.VMEM((D_EXPERT, D_MODEL), BF16),
        ],
    )
    return pl.pallas_call(
        _moe_kernel,
        grid_spec=grid_spec,
        out_shape=jax.ShapeDtypeStruct((p_rows * ROW_PITCH, LANES), F32),
        compiler_params=_cparams(("arbitrary",)),
        name="moe_ffn",
    )(tile_e, tile_first, tile_next, tile_wslot, n_used, src_row, xnew, gffn, w_gate, w_up, w_down)


def _combine_kernel(pos_ref, xnew_ref, wts_ref, y_hbm, out0_ref, out1_ref, ybuf, sem, *, nt0, n_tok):
    i = pl.program_id(0)
    nt = pl.num_programs(0)
    tm = xnew_ref.shape[0] // ROW_PITCH
    slot = lax.rem(i, 2)

    def start(tile, s):
        _row_gather_start(y_hbm, pos_ref, tile * tm, tm, ybuf.at[s, 0], sem.at[s])
        _row_gather_start(y_hbm, pos_ref, n_tok + tile * tm, tm, ybuf.at[s, 1], sem.at[s], priority=1)

    @pl.when(i == 0)
    def _():
        start(0, 0)

    @pl.when(i + 1 < nt)
    def _():
        start(i + 1, 1 - slot)

    _row_gather_wait(y_hbm, tm, ybuf.at[slot, 0], sem.at[slot])
    _row_gather_wait(y_hbm, tm, ybuf.at[slot, 1], sem.at[slot])
    wt = wts_ref[...].T
    w1 = jnp.broadcast_to(wt[:, 0:1], (tm, LANES))
    w2 = jnp.broadcast_to(wt[:, 1:2], (tm, LANES))

    def write(out_ref):
        for s in range(ROW_TILES):
            out_ref[:, s * LANES:(s + 1) * LANES] = (
                _token_slab(xnew_ref, tm, s)
                + (w1 * _token_slab(ybuf.at[slot, 0], tm, s) + w2 * _token_slab(ybuf.at[slot, 1], tm, s)))

    @pl.when(i < nt0)
    def _():
        write(out0_ref)

    @pl.when(i >= nt0)
    def _():
        write(out1_ref)


def _combine(xnew, y_sorted, wts, pos, t0, *, tm):
    t = xnew.shape[0] // ROW_PITCH
    nt0 = t0 // tm
    nt = t // tm
    grid_spec = pltpu.PrefetchScalarGridSpec(
        num_scalar_prefetch=1,
        grid=(nt,),
        in_specs=[
            pl.BlockSpec((tm * ROW_PITCH, LANES), lambda i, ps: (i, 0)),
            pl.BlockSpec((8, tm), lambda i, ps: (0, i)),
            pl.BlockSpec(memory_space=pl.ANY),
        ],
        out_specs=[
            pl.BlockSpec((tm, D_MODEL), lambda i, ps: (jnp.minimum(i, nt0 - 1), 0)),
            pl.BlockSpec((tm, D_MODEL), lambda i, ps: (jnp.maximum(i - nt0, 0), 0)),
        ],
        scratch_shapes=[
            pltpu.VMEM((2, 2, tm * ROW_PITCH, LANES), F32),
            pltpu.SemaphoreType.DMA((2,)),
        ],
    )
    return pl.pallas_call(
        functools.partial(_combine_kernel, nt0=nt0, n_tok=t),
        grid_spec=grid_spec,
        out_shape=[
            jax.ShapeDtypeStruct((t0, D_MODEL), F32),
            jax.ShapeDtypeStruct((t - t0, D_MODEL), F32),
        ],
        compiler_params=_cparams(("arbitrary",)),
        name="moe_combine",
    )(pos, xnew, wts, y_sorted)


def _rope_tables(max_len):
    inv = 1.0 / (ROPE_THETA ** (jnp.arange(0, HEAD_DIM, 2, dtype=F32) / HEAD_DIM))
    ang = jnp.arange(max_len, dtype=F32)[:, None] * inv[None, :]
    ang = jnp.concatenate([ang, ang, ang, ang], axis=-1)
    lane = jnp.arange(LANES)
    sign = jnp.where((lane % HEAD_DIM) < HEAD_DIM // 2, -1.0, 1.0).astype(F32)
    return jnp.cos(ang), jnp.sin(ang) * sign[None, :]


def _tile_tables(groups, tm):
    pos, length = [], []
    for batch, s_len in groups:
        for _ in range(batch):
            for p in range(0, s_len, tm):
                pos.append(p)
                length.append(s_len)
    return np.asarray(pos, np.int32), np.asarray(length, np.int32)


def _routing_tables(ids, n_tok, tm):
    e_flat = ids[0:2, :].reshape(-1)
    onehot = (e_flat[:, None] == jnp.arange(N_EXPERTS, dtype=I32)[None, :]).astype(I32)
    csum = jnp.cumsum(onehot, axis=0)
    rank = jnp.sum(csum * onehot, axis=1) - 1
    counts = csum[-1]
    padded = ((counts + tm - 1) // tm) * tm
    ends = jnp.cumsum(padded)
    starts = ends - padded
    dst = (starts[e_flat] + rank).astype(I32)
    p_rows = 2 * n_tok + N_EXPERTS * tm
    n_tiles = p_rows // tm
    n_used = (ends[-1] // tm).astype(I32)
    tile_start = jnp.arange(n_tiles, dtype=I32) * tm
    tile_e = jnp.sum((ends[None, :] <= tile_start[:, None]).astype(I32), axis=1)
    tile_e = jnp.minimum(tile_e, N_EXPERTS - 1)
    active = jnp.arange(n_tiles, dtype=I32) < n_used
    order = jnp.argsort(e_flat, stable=True).astype(I32)
    shift = (starts - (jnp.cumsum(counts) - counts))[tile_e]
    valid_rows = jnp.where(active, jnp.minimum(counts[tile_e] - (tile_start - starts[tile_e]), tm), 0)
    slot = jnp.arange(p_rows, dtype=I32).reshape(n_tiles, tm)
    pair = order[jnp.clip(slot - shift[:, None], 0, 2 * n_tok - 1)]
    in_tile = jnp.arange(tm, dtype=I32)[None, :] < valid_rows[:, None]
    src_row = jnp.where(in_tile, lax.rem(pair, n_tok), 0).astype(I32).reshape(-1)
    last_e = tile_e[jnp.maximum(n_used - 1, 0)]
    tile_e = jnp.where(active, tile_e, last_e)
    tile_first = (active & (tile_start == starts[tile_e])).astype(I32)
    tile_wslot = lax.rem(jnp.cumsum(tile_first) - 1, 2).astype(I32)
    e_ids = jnp.arange(N_EXPERTS, dtype=I32)
    later_used = (counts[None, :] > 0) & (e_ids[None, :] > e_ids[:, None])
    next_used = jnp.min(jnp.where(later_used, e_ids[None, :], N_EXPERTS), axis=1)
    next_used = jnp.where(next_used == N_EXPERTS, -1, next_used).astype(I32)
    tile_next = next_used[tile_e]
    return tile_e, tile_first, tile_next, tile_wslot, n_used.reshape(1), src_row, dst


def _forward(x0, x1, groups, params, *, tm_proj, tn_proj, rc_proj, tq, tk, tm_post, tm_moe, tm_comb):
    (attn_norm_g, w_in, q_norm_g, k_norm_g, lambda_q1, lambda_k1, lambda_q2, lambda_k2, subln_g,
     w_attn_up, w_pool_grp, pool_scale, w_pool_up, w_out, ffn_norm_g, w_group_router,
     b_group_router, w_expert_router, b_expert_router, w_gate, w_up, w_down) = params
    t0, t1 = x0.shape[0], x1.shape[0]
    n_tok = t0 + t1
    lam_init = 0.8 - 0.6 * math.exp(-0.3 * 0)
    max_len = max(s for _, s in groups)

    cos_t, sin_t = _rope_tables(max_len)
    pos_proj, _ = _tile_tables(groups, tm_proj)
    q_scale = HEAD_DIM ** -0.5 * math.log2(math.e)
    qkg = jnp.zeros((8, ATTN_W), F32)
    qkg = qkg.at[0].set(jnp.tile(q_norm_g, ATTN_W // HEAD_DIM) * q_scale)
    qkg = qkg.at[1].set(jnp.tile(k_norm_g, ATTN_W // HEAD_DIM))
    seg = np.arange(LANES) // HEAD_DIM
    bd = jnp.asarray((seg[:, None] == seg[None, :]).astype(np.float32) / HEAD_DIM, BF16)

    proj = _in_projection(x0, x1, jnp.asarray(pos_proj // tm_proj), attn_norm_g[None, :],
                          w_in.astype(BF16), cos_t, sin_t, qkg, bd, tm=tm_proj, tn=tn_proj, rc=rc_proj)

    lam_pack = jnp.zeros((8, HEAD_DIM), F32)
    lam_pack = lam_pack.at[0].set(lambda_q1).at[1].set(lambda_k1).at[2].set(lambda_q2).at[3].set(lambda_k2)
    outs = []
    tok_off = 0
    for batch, s_len in groups:
        outs.append(_attention(proj, tok_off, batch, s_len, lam_pack, subln_g[None, :],
                               tq=min(tq, s_len), tk=min(tk, s_len), lam_init=lam_init))
        tok_off += batch * s_len

    tile_pos, tile_len = _tile_tables(groups, tm_post)
    w_router = jnp.zeros((D_MODEL, ROUTER_COLS), F32)
    w_router = w_router.at[:, 0:N_GROUPS].set(w_group_router)
    w_router = w_router.at[:, EXPERT_COL0:EXPERT_COL0 + N_EXPERTS].set(w_expert_router)
    b_router = jnp.zeros((1, ROUTER_COLS), F32)
    b_router = b_router.at[0, 0:N_GROUPS].set(b_group_router)
    b_router = b_router.at[0, EXPERT_COL0:EXPERT_COL0 + N_EXPERTS].set(b_expert_router)
    wr_hi = w_router.astype(BF16)
    wr_lo = (w_router - wr_hi.astype(F32)).astype(BF16)
    xnew, ids, wts = _post_block(
        x0, x1, outs[0], outs[1], proj, jnp.asarray(tile_pos), jnp.asarray(tile_len),
        w_attn_up.astype(BF16), w_pool_grp.astype(BF16), pool_scale[None, :], w_pool_up.astype(BF16),
        w_out.astype(BF16), ffn_norm_g[None, :], jnp.concatenate([wr_hi, wr_lo], axis=1), b_router,
        tm=tm_post)

    tile_e, tile_first, tile_next, tile_wslot, n_used, src_row, dst = _routing_tables(ids, n_tok, tm_moe)
    y_sorted = _moe_ffn(
        xnew, tile_e, tile_first, tile_next, tile_wslot, n_used, src_row, ffn_norm_g[None, :],
        w_gate.reshape(N_EXPERTS, D_MODEL, D_EXPERT), w_up.reshape(N_EXPERTS, D_MODEL, D_EXPERT),
        w_down.reshape(N_EXPERTS, D_EXPERT, D_MODEL), tm=tm_moe)
    return _combine(xnew, y_sorted, wts, dst, t0, tm=tm_comb)


def kernel(x_prompt, x_sample, attn_norm_g, w_in, q_norm_g, k_norm_g, lambda_q1, lambda_k1, lambda_q2,
           lambda_k2, subln_g, w_attn_up, w_pool_grp, pool_scale, w_pool_up, w_out, ffn_norm_g,
           w_group_router, b_group_router, w_expert_router, b_expert_router, w_gate, w_up, w_down):
    assert attn_norm_g.shape[0] == 1, "single-layer stack"
    params = tuple(p[0] for p in (
        attn_norm_g, w_in, q_norm_g, k_norm_g, lambda_q1, lambda_k1, lambda_q2, lambda_k2, subln_g,
        w_attn_up, w_pool_grp, pool_scale, w_pool_up, w_out, ffn_norm_g, w_group_router,
        b_group_router, w_expert_router, b_expert_router, w_gate, w_up, w_down))
    groups = (x_prompt.shape[:2], x_sample.shape[:2])
    y0, y1 = _forward(
        x_prompt.reshape(-1, D_MODEL), x_sample.reshape(-1, D_MODEL), groups, params,
        tm_proj=1024, tn_proj=1024, rc_proj=512, tq=4096, tk=512, tm_post=256, tm_moe=256, tm_comb=256)
    return y0.reshape(x_prompt.shape), y1.reshape(x_sample.shape)
```

```python
import functools
import math

import numpy as np
import jax
import jax.numpy as jnp
from jax import lax
from jax.experimental import pallas as pl
from jax.experimental.pallas import tpu as pltpu

F32 = jnp.float32
BF16 = jnp.bfloat16
I32 = jnp.int32

D_MODEL = 2048
HEAD_DIM = 64
N_HEADS = 8
HEAD_W = 2 * HEAD_DIM
ATTN_W = N_HEADS * HEAD_W
POOL_W = 1024
POOL_WINDOWS = (2, 4, 8, 16)
POOL_GROUP_W = POOL_W // len(POOL_WINDOWS)
POOL_HALO = 64
IN_COLS = 3 * ATTN_W + POOL_W + 2 * D_MODEL
N_GROUPS = 4
EXPERTS_PER_GROUP = 8
N_EXPERTS = N_GROUPS * EXPERTS_PER_GROUP
D_EXPERT = 512
ROPE_THETA = 10000.0
EPS = 1e-6
LANES = 128
ROUTER_COLS = 128
EXPERT_COL0 = 8
ROW_TILES = D_MODEL // LANES
ROW_PITCH = ROW_TILES + 1
SAFE_SCORE_BOUND = 50.0
SCORE_BOUND_MARGIN = 1.02

COL_Q, COL_K, COL_V, COL_PIN, COL_GA, COL_GP = 0, 1, 2, 3, 4, 6

VMEM_LIMIT = 56 * 1024 * 1024


def _cparams(sem, vmem=VMEM_LIMIT):
    return pltpu.CompilerParams(dimension_semantics=sem, vmem_limit_bytes=vmem)


def _token_slab(ref, n_tok, s):
    return ref[pl.ds(s, n_tok, stride=ROW_PITCH), :]


def _load_token_rows(ref, n_tok):
    return jnp.concatenate([_token_slab(ref, n_tok, s) for s in range(ROW_TILES)], axis=1)


def _store_token_rows(ref, value):
    n_tok = value.shape[0]
    for s in range(ROW_TILES):
        ref[pl.ds(s, n_tok, stride=ROW_PITCH), :] = value[:, s * LANES:(s + 1) * LANES]
    ref[pl.ds(ROW_TILES, n_tok, stride=ROW_PITCH), :] = jnp.zeros((n_tok, LANES), ref.dtype)


def _inproj_kernel(posblk_ref, x0_hbm, x1_hbm, gattn_ref, w_ref, cos_ref, sin_ref, qkg_ref, bd_ref,
                   o_ref, xn_ref, xbuf, xsem, *, nt0, rc):
    del posblk_ref
    i = pl.program_id(0)
    j = pl.program_id(1)
    nt = pl.num_programs(0)
    tm = xn_ref.shape[0]
    n_chunks = tm // rc

    def x_tile_copy(x_hbm, tile):
        return pltpu.make_async_copy(x_hbm.at[pl.ds(tile * tm, tm)], xbuf, xsem)

    def start_x_tile(tile):
        @pl.when(tile < nt0)
        def _():
            x_tile_copy(x0_hbm, tile).start()

        @pl.when(tile >= nt0)
        def _():
            x_tile_copy(x1_hbm, tile - nt0).start()

    @pl.when(j == 0)
    def _():
        @pl.when(i == 0)
        def _():
            start_x_tile(i)

        x_tile_copy(x0_hbm, 0).wait()
        for c in range(n_chunks):
            xs = xbuf[c * rc:(c + 1) * rc, :]
            ms = jnp.mean(xs * xs, axis=-1, keepdims=True)
            xn_ref[c * rc:(c + 1) * rc, :] = (xs * lax.rsqrt(ms + EPS) * gattn_ref[...]).astype(BF16)

        @pl.when(i + 1 < nt)
        def _():
            start_x_tile(i + 1)

    def chunk_acc(c):
        return jnp.dot(xn_ref[c * rc:(c + 1) * rc, :], w_ref[...], preferred_element_type=F32)

    @pl.when(j <= COL_K)
    def _():
        g = jnp.where(j == COL_Q, qkg_ref[0:1, :], qkg_ref[1:2, :])
        lane = lax.broadcasted_iota(I32, (rc, LANES), 1)
        first_half = (lane % HEAD_DIM) < (HEAD_DIM // 2)
        for c in range(n_chunks):
            a = chunk_acc(c)
            cos = cos_ref[c * rc:(c + 1) * rc, :]
            sin = sin_ref[c * rc:(c + 1) * rc, :]
            for t in range(a.shape[1] // LANES):
                at = a[:, t * LANES:(t + 1) * LANES]
                ms = jnp.dot((at * at).astype(BF16), bd_ref[...], preferred_element_type=F32)
                u = at * g[:, t * LANES:(t + 1) * LANES]
                rot = jnp.where(first_half, pltpu.roll(u, LANES - HEAD_DIM // 2, 1),
                                pltpu.roll(u, HEAD_DIM // 2, 1))
                val = (u * cos + rot * sin) * lax.rsqrt(ms + EPS)
                o_ref[c * rc:(c + 1) * rc, t * LANES:(t + 1) * LANES] = val.astype(BF16)

    @pl.when((j == COL_V) | (j == COL_PIN))
    def _():
        for c in range(n_chunks):
            o_ref[c * rc:(c + 1) * rc, :] = chunk_acc(c).astype(BF16)

    @pl.when(j >= COL_GA)
    def _():
        for c in range(n_chunks):
            o_ref[c * rc:(c + 1) * rc, :] = (0.5 * jnp.tanh(0.5 * chunk_acc(c)) + 0.5).astype(BF16)


def _in_projection(x0, x1, posblk, gattn, w_in, cos_t, sin_t, qkg, bd, *, tm, tn, rc):
    t0, t1 = x0.shape[0], x1.shape[0]
    nt0, nt1 = t0 // tm, t1 // tm
    nt = nt0 + nt1
    grid_spec = pltpu.PrefetchScalarGridSpec(
        num_scalar_prefetch=1,
        grid=(nt, IN_COLS // tn),
        in_specs=[
            pl.BlockSpec(memory_space=pl.ANY),
            pl.BlockSpec(memory_space=pl.ANY),
            pl.BlockSpec((1, D_MODEL), lambda i, j, pb: (0, 0)),
            pl.BlockSpec((D_MODEL, tn), lambda i, j, pb: (0, j)),
            pl.BlockSpec((tm, LANES), lambda i, j, pb: (pb[i], 0)),
            pl.BlockSpec((tm, LANES), lambda i, j, pb: (pb[i], 0)),
            pl.BlockSpec((8, tn), lambda i, j, pb: (0, 0)),
            pl.BlockSpec((LANES, LANES), lambda i, j, pb: (0, 0)),
        ],
        out_specs=pl.BlockSpec((tm, tn), lambda i, j, pb: (i, j)),
        scratch_shapes=[
            pltpu.VMEM((tm, D_MODEL), BF16),
            pltpu.VMEM((tm, D_MODEL), F32),
            pltpu.SemaphoreType.DMA(()),
        ],
    )
    return pl.pallas_call(
        functools.partial(_inproj_kernel, nt0=nt0, rc=rc),
        grid_spec=grid_spec,
        out_shape=jax.ShapeDtypeStruct((t0 + t1, IN_COLS), BF16),
        compiler_params=_cparams(("arbitrary", "arbitrary")),
        name="in_projection",
    )(posblk, x0, x1, gattn, w_in, cos_t, sin_t, qkg, bd)


def _attn_kernel(q_ref, k_ref, v_ref, lam_ref, sg_ref, o_ref, vt_ref, acc_ref, m_ref, l_ref, kn_ref,
                 p_ref, *, tk, lam_init):
    i = pl.program_id(2)
    s_len = k_ref.shape[0]
    tq = q_ref.shape[0]
    nk = s_len // tk
    nt_dims = (((1,), (1,)), ((), ()))

    sel_r = lax.broadcasted_iota(I32, (8, HEAD_W), 0)
    sel_l = lax.broadcasted_iota(I32, (8, HEAD_W), 1)
    sel = jnp.where(sel_l // HEAD_DIM == sel_r, 1.0, 0.0).astype(BF16)

    @pl.when(i == 0)
    def _():
        kn2 = jnp.zeros((8, 1), F32)
        for jb in range(nk):
            rows = slice(jb * tk, (jb + 1) * tk)
            vt_ref[:, rows] = v_ref[rows, :].astype(F32).T.astype(BF16)
            kf = k_ref[rows, :].astype(F32)
            n2 = lax.dot_general(sel, (kf * kf).astype(BF16), nt_dims, preferred_element_type=F32)
            kn2 = jnp.maximum(kn2, jnp.max(n2, axis=1, keepdims=True))
        kn_ref[...] = jnp.broadcast_to(kn2, kn_ref.shape)

    q = q_ref[...]
    lane = lax.broadcasted_iota(I32, q.shape, 1)
    zero = jnp.zeros_like(q)
    qc = (jnp.where(lane < HEAD_DIM, q, zero), jnp.where(lane >= HEAD_DIM, q, zero))

    qf = q.astype(F32)
    qn2 = lax.dot_general(sel, (qf * qf).astype(BF16), nt_dims, preferred_element_type=F32)
    bound = jnp.sqrt(qn2 * kn_ref[:, 0:1]) * SCORE_BOUND_MARGIN
    bound_c = (bound[0:1, :], bound[1:2, :])
    use_bound = jnp.max(bound[0:2, :]) <= SAFE_SCORE_BOUND

    l_ref[...] = jnp.zeros(l_ref.shape, F32)
    acc_ref[...] = jnp.zeros(acc_ref.shape, F32)

    def load_blocks(jb):
        off = jb * tk if isinstance(jb, int) else pl.multiple_of(jb * tk, tk)
        return k_ref[pl.ds(off, tk), :], vt_ref[:, pl.ds(off, tk)]

    def bounded_weights(jb, slot):
        kb, _ = load_blocks(jb)
        for c in range(2):
            st = lax.dot_general(kb, qc[c], nt_dims, preferred_element_type=F32)
            p = jnp.exp2(st - bound_c[c])
            l_ref[c] += jnp.sum(p, axis=0, keepdims=True)
            p_ref[slot, c] = p.astype(BF16)

    def weighted_values(jb, slot):
        _, vtb = load_blocks(jb)
        for c in range(2):
            acc_ref[c] += jnp.dot(vtb, p_ref[slot, c], preferred_element_type=F32)

    def bounded_pair(m, carry):
        j0 = 2 * m
        bounded_weights(j0 + 1, 1)
        weighted_values(j0, 0)
        bounded_weights(j0 + 2, 0)
        weighted_values(j0 + 1, 1)
        return carry

    def running_max_step(jb, carry):
        kb, vtb = load_blocks(jb)
        for c in range(2):
            st = lax.dot_general(kb, qc[c], nt_dims, preferred_element_type=F32)
            m_old = m_ref[c]
            m_new = jnp.maximum(m_old, jnp.max(st, axis=0, keepdims=True))
            alpha = jnp.exp2(m_old - m_new)
            p = jnp.exp2(st - m_new)
            l_ref[c] = alpha * l_ref[c] + jnp.sum(p, axis=0, keepdims=True)
            acc_ref[c] = alpha * acc_ref[c] + jnp.dot(vtb, p.astype(BF16), preferred_element_type=F32)
            m_ref[c] = m_new
        return carry

    @pl.when(use_bound)
    def _():
        bounded_weights(0, 0)
        lax.fori_loop(0, nk // 2 - 1, bounded_pair, 0)
        bounded_weights(nk - 1, 1)
        weighted_values(nk - 2, 0)
        weighted_values(nk - 1, 1)

    @pl.when(jnp.logical_not(use_bound))
    def _():
        m_ref[...] = jnp.full(m_ref.shape, -jnp.inf, F32)
        lax.fori_loop(0, nk, running_max_step, 0)

    lq1, lk1, lq2, lk2 = lam_ref[0:1, :], lam_ref[1:2, :], lam_ref[2:3, :], lam_ref[3:4, :]
    lam = (jnp.exp(jnp.sum(lq1 * lk1, axis=-1, keepdims=True))
           - jnp.exp(jnp.sum(lq2 * lk2, axis=-1, keepdims=True)) + lam_init)
    ot = acc_ref[0] / l_ref[0] - lam * (acc_ref[1] / l_ref[1])
    ms = jnp.mean(ot * ot, axis=0, keepdims=True)
    on = (ot * lax.rsqrt(ms + EPS)).T
    o_ref[...] = (on * sg_ref[...] * (1.0 - lam_init)).astype(BF16)


def _attention(proj, tok_off, batch, s_len, lam_pack, subln_g, *, tq, tk, lam_init):
    qb0 = tok_off // tq
    sb0 = tok_off // s_len
    nq = s_len // tq
    return pl.pallas_call(
        functools.partial(_attn_kernel, tk=tk, lam_init=lam_init),
        grid=(batch, N_HEADS, nq),
        in_specs=[
            pl.BlockSpec((tq, HEAD_W), lambda b, h, i: (qb0 + b * nq + i, COL_Q * N_HEADS + h)),
            pl.BlockSpec((s_len, HEAD_W), lambda b, h, i: (sb0 + b, COL_K * N_HEADS + h)),
            pl.BlockSpec((s_len, HEAD_W), lambda b, h, i: (sb0 + b, COL_V * N_HEADS + h)),
            pl.BlockSpec((8, HEAD_DIM), lambda b, h, i: (0, 0)),
            pl.BlockSpec((1, HEAD_W), lambda b, h, i: (0, 0)),
        ],
        out_specs=pl.BlockSpec((tq, HEAD_W), lambda b, h, i: (b * nq + i, h)),
        out_shape=jax.ShapeDtypeStruct((batch * s_len, ATTN_W), BF16),
        scratch_shapes=[
            pltpu.VMEM((HEAD_W, s_len), BF16),
            pltpu.VMEM((2, HEAD_W, tq), F32),
            pltpu.VMEM((2, 1, tq), F32),
            pltpu.VMEM((2, 1, tq), F32),
            pltpu.VMEM((8, LANES), F32),
            pltpu.VMEM((2, 2, tk, tq), BF16),
        ],
        compiler_params=_cparams(("arbitrary", "arbitrary", "arbitrary")),
        name="diff_attention",
    )(proj, proj, proj, lam_pack, subln_g)


def _post_kernel(pos_ref, len_ref, x0_ref, x1_ref, o0_ref, o1_ref, pin_ref, prev_ref, next_ref,
                 ga_ref, gp_ref, wau_ref, wgrp_ref, pscale_ref, wpu_ref, wout_ref, gffn_ref,
                 wrc_ref, br_ref, xnew_ref, ids_ref, wts_ref, *, nt0):
    i = pl.program_id(0)
    tm = pin_ref.shape[0]
    p0 = pos_ref[i]
    s_len = len_ref[i]
    first = i < nt0

    o = jnp.where(first, o0_ref[...], o1_ref[...])
    attn_d = jnp.dot(o, wau_ref[...], preferred_element_type=F32)

    ext = jnp.concatenate([prev_ref[...], pin_ref[...], next_ref[...]], axis=0)
    ke = tm + 2 * POOL_HALO
    r_band = lax.broadcasted_iota(I32, (tm, ke), 0) + p0
    c_band = lax.broadcasted_iota(I32, (tm, ke), 1) + (p0 - POOL_HALO)
    r_cnt = lax.broadcasted_iota(I32, (tm, POOL_GROUP_W), 0) + p0
    pooled = []
    for gi, w in enumerate(POOL_WINDOWS):
        back, fwd = w // 2, w - 1 - w // 2
        lo = jnp.maximum(r_band - back, 0)
        hi = jnp.minimum(r_band + fwd, s_len - 1)
        band = jnp.where((c_band >= lo) & (c_band <= hi), 1.0, 0.0).astype(BF16)
        cnt = jnp.minimum(r_cnt + fwd, s_len - 1) - jnp.maximum(r_cnt - back, 0) + 1
        cols = slice(gi * POOL_GROUP_W, (gi + 1) * POOL_GROUP_W)
        wsum = jnp.dot(band, ext[:, cols], preferred_element_type=F32)
        y = wsum / cnt.astype(F32) - pin_ref[:, cols].astype(F32)
        pooled.append(jnp.dot(y.astype(BF16), wgrp_ref[gi], preferred_element_type=F32))
    pooled = (jnp.concatenate(pooled, axis=1) * pscale_ref[...]).astype(BF16)
    pool_d = jnp.dot(pooled, wpu_ref[...], preferred_element_type=F32)

    merged = ga_ref[...].astype(F32) * attn_d + gp_ref[...].astype(F32) * pool_d
    x = jnp.where(first, x0_ref[...], x1_ref[...])
    xnew = x + jnp.dot(merged.astype(BF16), wout_ref[...], preferred_element_type=F32)
    _store_token_rows(xnew_ref, xnew)

    ms = jnp.mean(xnew * xnew, axis=-1, keepdims=True)
    hn = xnew * lax.rsqrt(ms + EPS) * gffn_ref[...]
    hi_part = hn.astype(BF16)
    lo_part = (hn - hi_part.astype(F32)).astype(BF16)
    hw = jnp.dot(hi_part, wrc_ref[...], preferred_element_type=F32)
    logits = (hw[:, 0:ROUTER_COLS]
              + (hw[:, ROUTER_COLS:]
                 + jnp.dot(lo_part, wrc_ref[:, 0:ROUTER_COLS], preferred_element_type=F32))
              + br_ref[...])
    lt = logits.T

    gl = lt[0:N_GROUPS, :]
    g_iota = lax.broadcasted_iota(I32, gl.shape, 0)
    g_max = jnp.max(gl, axis=0, keepdims=True)
    g_sel = jnp.min(jnp.where(gl == g_max, g_iota, N_GROUPS), axis=0, keepdims=True)
    g_w = 1.0 / jnp.sum(jnp.exp(gl - g_max), axis=0, keepdims=True)
    el = lt[EXPERT_COL0:EXPERT_COL0 + N_EXPERTS, :]
    e_iota = lax.broadcasted_iota(I32, el.shape, 0)
    neg = jnp.float32(-jnp.inf)
    el = jnp.where(e_iota // EXPERTS_PER_GROUP == g_sel, el, neg)
    v1 = jnp.max(el, axis=0, keepdims=True)
    i1 = jnp.min(jnp.where(el == v1, e_iota, N_EXPERTS), axis=0, keepdims=True)
    el2 = jnp.where(e_iota == i1, neg, el)
    v2 = jnp.max(el2, axis=0, keepdims=True)
    i2 = jnp.min(jnp.where(el2 == v2, e_iota, N_EXPERTS), axis=0, keepdims=True)
    d = jnp.exp(v2 - v1)
    w1 = g_w / (1.0 + d)
    w2 = g_w * d / (1.0 + d)
    row = lax.broadcasted_iota(I32, (8, tm), 0)
    ids_ref[...] = jnp.where(row == 0, i1, jnp.where(row == 1, i2, 0))
    wts_ref[...] = jnp.where(row == 0, w1, jnp.where(row == 1, w2, 0.0))


def _post_block(x0, x1, o0, o1, proj, tile_pos, tile_len, wau, wgrp, pscale, wpu, wout, gffn,
                wrc, br, *, tm):
    t0, t1 = x0.shape[0], x1.shape[0]
    t = t0 + t1
    nt0 = t0 // tm
    nt = t // tm
    hb = tm // POOL_HALO
    n_hblk = t // POOL_HALO
    const = lambda *shape: pl.BlockSpec(shape, lambda i, tp, tl: (0,) * len(shape),
                                        pipeline_mode=pl.Buffered(1))
    grid_spec = pltpu.PrefetchScalarGridSpec(
        num_scalar_prefetch=2,
        grid=(nt,),
        in_specs=[
            pl.BlockSpec((tm, D_MODEL), lambda i, tp, tl: (jnp.minimum(i, nt0 - 1), 0)),
            pl.BlockSpec((tm, D_MODEL), lambda i, tp, tl: (jnp.maximum(i - nt0, 0), 0)),
            pl.BlockSpec((tm, ATTN_W), lambda i, tp, tl: (jnp.minimum(i, nt0 - 1), 0)),
            pl.BlockSpec((tm, ATTN_W), lambda i, tp, tl: (jnp.maximum(i - nt0, 0), 0)),
            pl.BlockSpec((tm, POOL_W), lambda i, tp, tl: (i, COL_PIN)),
            pl.BlockSpec((POOL_HALO, POOL_W), lambda i, tp, tl: (jnp.maximum(i * hb - 1, 0), COL_PIN)),
            pl.BlockSpec((POOL_HALO, POOL_W), lambda i, tp, tl: (jnp.minimum((i + 1) * hb, n_hblk - 1), COL_PIN)),
            pl.BlockSpec((tm, D_MODEL), lambda i, tp, tl: (i, COL_GA // 2)),
            pl.BlockSpec((tm, D_MODEL), lambda i, tp, tl: (i, COL_GP // 2)),
            const(ATTN_W, D_MODEL),
            const(len(POOL_WINDOWS), POOL_GROUP_W, POOL_GROUP_W),
            const(1, POOL_W),
            const(POOL_W, D_MODEL),
            const(D_MODEL, D_MODEL),
            const(1, D_MODEL),
            const(D_MODEL, 2 * ROUTER_COLS),
            const(1, ROUTER_COLS),
        ],
        out_specs=[
            pl.BlockSpec((tm * ROW_PITCH, LANES), lambda i, tp, tl: (i, 0)),
            pl.BlockSpec((8, tm), lambda i, tp, tl: (0, i)),
            pl.BlockSpec((8, tm), lambda i, tp, tl: (0, i)),
        ],
    )
    return pl.pallas_call(
        functools.partial(_post_kernel, nt0=nt0),
        grid_spec=grid_spec,
        out_shape=[
            jax.ShapeDtypeStruct((t * ROW_PITCH, LANES), F32),
            jax.ShapeDtypeStruct((8, t), I32),
            jax.ShapeDtypeStruct((8, t), F32),
        ],
        compiler_params=_cparams(("arbitrary",)),
        name="post_block",
    )(tile_pos, tile_len, x0, x1, o0, o1, proj, proj, proj, proj, proj, wau, wgrp, pscale, wpu,
      wout, gffn, wrc, br)


GATHER_UNROLL = 8
WEIGHT_DMA_PRIORITY = 1


def _row_gather_start(src_hbm, idx_ref, base, n_rows, buf, sem, priority=0):
    def body(r, carry):
        src0 = idx_ref[base + r] * ROW_PITCH
        dst0 = r * ROW_PITCH
        pltpu.make_async_copy(src_hbm.at[pl.ds(src0, ROW_TILES)], buf.at[pl.ds(dst0, ROW_TILES)],
                              sem).start(priority=priority)
        return carry
    lax.fori_loop(0, n_rows, body, 0, unroll=GATHER_UNROLL)


def _row_gather_wait(src_hbm, n_rows, buf, sem):
    n = n_rows * ROW_TILES
    pltpu.make_async_copy(src_hbm.at[pl.ds(0, n)], buf.at[pl.ds(0, n)], sem).wait()


def _moe_kernel(te_ref, tf_ref, tn_ref, ts_ref, nu_ref, src_ref, x_hbm, gffn_ref, wg_hbm, wu_hbm, wd_hbm,
                y_ref, xbuf, sem, wgf, wuf, wdf, wsem, wgb, wub, wdb):
    i = pl.program_id(0)
    tm = y_ref.shape[0] // ROW_PITCH
    n_used = nu_ref[0]
    slot = lax.rem(i, 2)

    def weight_copies(e, s):
        return (pltpu.make_async_copy(wg_hbm.at[e], wgf.at[s], wsem.at[s]),
                pltpu.make_async_copy(wu_hbm.at[e], wuf.at[s], wsem.at[s]),
                pltpu.make_async_copy(wd_hbm.at[e], wdf.at[s], wsem.at[s]))

    @pl.when(i == 0)
    def _():
        _row_gather_start(x_hbm, src_ref, 0, tm, xbuf.at[0], sem.at[0])
        for cp in weight_copies(te_ref[0], 0):
            cp.start(priority=WEIGHT_DMA_PRIORITY)

    @pl.when(i + 1 < n_used)
    def _():
        _row_gather_start(x_hbm, src_ref, (i + 1) * tm, tm, xbuf.at[1 - slot], sem.at[1 - slot])

    @pl.when(i < n_used)
    def _():
        @pl.when(tf_ref[i] == 1)
        def _():
            ws = ts_ref[i]
            for cp in weight_copies(te_ref[i], ws):
                cp.wait()
            for s in range(2):
                @pl.when(ws == s)
                def _(s=s):
                    wgb[...] = wgf[s].astype(BF16)
                    wub[...] = wuf[s].astype(BF16)
                    wdb[...] = wdf[s].astype(BF16)

            @pl.when(tn_ref[i] >= 0)
            def _():
                for cp in weight_copies(tn_ref[i], 1 - ws):
                    cp.start(priority=WEIGHT_DMA_PRIORITY)

        _row_gather_wait(x_hbm, tm, xbuf.at[slot], sem.at[slot])
        xs = _load_token_rows(xbuf.at[slot], tm)
        ms = jnp.mean(xs * xs, axis=-1, keepdims=True)
        hn = (xs * lax.rsqrt(ms + EPS) * gffn_ref[...]).astype(BF16)
        gate = jnp.dot(hn, wgb[...], preferred_element_type=F32)
        up = jnp.dot(hn, wub[...], preferred_element_type=F32)
        h = jax.nn.silu(gate) * up
        _store_token_rows(y_ref, jnp.dot(h.astype(BF16), wdb[...], preferred_element_type=F32))

    @pl.when(i >= n_used)
    def _():
        y_ref[...] = jnp.zeros(y_ref.shape, y_ref.dtype)


def _moe_ffn(xnew, tile_e, tile_first, tile_next, tile_wslot, n_used, src_row, gffn, w_gate, w_up, w_down,
             *, tm):
    p_rows = src_row.shape[0]
    n_tiles = p_rows // tm
    grid_spec = pltpu.PrefetchScalarGridSpec(
        num_scalar_prefetch=6,
        grid=(n_tiles,),
        in_specs=[
            pl.BlockSpec(memory_space=pl.ANY),
            pl.BlockSpec((1, D_MODEL), lambda i, *_: (0, 0)),
            pl.BlockSpec(memory_space=pl.ANY),
            pl.BlockSpec(memory_space=pl.ANY),
            pl.BlockSpec(memory_space=pl.ANY),
        ],
        out_specs=pl.BlockSpec((tm * ROW_PITCH, LANES), lambda i, *_: (i, 0)),
        scratch_shapes=[
            pltpu.VMEM((2, tm * ROW_PITCH, LANES), F32),
            pltpu.SemaphoreType.DMA((2,)),
            pltpu.VMEM((2, D_MODEL, D_EXPERT), F32),
            pltpu.VMEM((2, D_MODEL, D_EXPERT), F32),
            pltpu.VMEM((2, D_EXPERT, D_MODEL), F32),
            pltpu.SemaphoreType.DMA((2,)),
            pltpu.VMEM((D_MODEL, D_EXPERT), BF16),
            pltpu.VMEM((D_MODEL, D_EXPERT), BF16),
            pltpu.VMEM((D_EXPERT, D_MODEL), BF16),
        ],
    )
    return pl.pallas_call(
        _moe_kernel,
        grid_spec=grid_spec,
        out_shape=jax.ShapeDtypeStruct((p_rows * ROW_PITCH, LANES), F32),
        compiler_params=_cparams(("arbitrary",)),
        name="moe_ffn",
    )(tile_e, tile_first, tile_next, tile_wslot, n_used, src_row, xnew, gffn, w_gate, w_up, w_down)


def _combine_kernel(pos_ref, xnew_ref, wts_ref, y_hbm, out0_ref, out1_ref, ybuf, sem, *, nt0, n_tok):
    i = pl.program_id(0)
    nt = pl.num_programs(0)
    tm = xnew_ref.shape[0] // ROW_PITCH
    slot = lax.rem(i, 2)

    def start(tile, s):
        _row_gather_start(y_hbm, pos_ref, tile * tm, tm, ybuf.at[s, 0], sem.at[s])
        _row_gather_start(y_hbm, pos_ref, n_tok + tile * tm, tm, ybuf.at[s, 1], sem.at[s], priority=1)

    @pl.when(i == 0)
    def _():
        start(0, 0)

    @pl.when(i + 1 < nt)
    def _():
        start(i + 1, 1 - slot)

    _row_gather_wait(y_hbm, tm, ybuf.at[slot, 0], sem.at[slot])
    _row_gather_wait(y_hbm, tm, ybuf.at[slot, 1], sem.at[slot])
    wt = wts_ref[...].T
    w1 = jnp.broadcast_to(wt[:, 0:1], (tm, LANES))
    w2 = jnp.broadcast_to(wt[:, 1:2], (tm, LANES))

    def write(out_ref):
        for s in range(ROW_TILES):
            out_ref[:, s * LANES:(s + 1) * LANES] = (
                _token_slab(xnew_ref, tm, s)
                + (w1 * _token_slab(ybuf.at[slot, 0], tm, s) + w2 * _token_slab(ybuf.at[slot, 1], tm, s)))

    @pl.when(i < nt0)
    def _():
        write(out0_ref)

    @pl.when(i >= nt0)
    def _():
        write(out1_ref)


def _combine(xnew, y_sorted, wts, pos, t0, *, tm):
    t = xnew.shape[0] // ROW_PITCH
    nt0 = t0 // tm
    nt = t // tm
    grid_spec = pltpu.PrefetchScalarGridSpec(
        num_scalar_prefetch=1,
        grid=(nt,),
        in_specs=[
            pl.BlockSpec((tm * ROW_PITCH, LANES), lambda i, ps: (i, 0)),
            pl.BlockSpec((8, tm), lambda i, ps: (0, i)),
            pl.BlockSpec(memory_space=pl.ANY),
        ],
        out_specs=[
            pl.BlockSpec((tm, D_MODEL), lambda i, ps: (jnp.minimum(i, nt0 - 1), 0)),
            pl.BlockSpec((tm, D_MODEL), lambda i, ps: (jnp.maximum(i - nt0, 0), 0)),
        ],
        scratch_shapes=[
            pltpu.VMEM((2, 2, tm * ROW_PITCH, LANES), F32),
            pltpu.SemaphoreType.DMA((2,)),
        ],
    )
    return pl.pallas_call(
        functools.partial(_combine_kernel, nt0=nt0, n_tok=t),
        grid_spec=grid_spec,
        out_shape=[
            jax.ShapeDtypeStruct((t0, D_MODEL), F32),
            jax.ShapeDtypeStruct((t - t0, D_MODEL), F32),
        ],
        compiler_params=_cparams(("arbitrary",)),
        name="moe_combine",
    )(pos, xnew, wts, y_sorted)


def _rope_tables(max_len):
    inv = 1.0 / (ROPE_THETA ** (jnp.arange(0, HEAD_DIM, 2, dtype=F32) / HEAD_DIM))
    ang = jnp.arange(max_len, dtype=F32)[:, None] * inv[None, :]
    ang = jnp.concatenate([ang, ang, ang, ang], axis=-1)
    lane = jnp.arange(LANES)
    sign = jnp.where((lane % HEAD_DIM) < HEAD_DIM // 2, -1.0, 1.0).astype(F32)
    return jnp.cos(ang), jnp.sin(ang) * sign[None, :]


def _tile_tables(groups, tm):
    pos, length = [], []
    for batch, s_len in groups:
        for _ in range(batch):
            for p in range(0, s_len, tm):
                pos.append(p)
                length.append(s_len)
    return np.asarray(pos, np.int32), np.asarray(length, np.int32)


def _routing_tables(ids, n_tok, tm):
    n_pair = 2 * n_tok
    e_flat = ids[0:2, :].reshape(-1)
    pair_ids = jnp.arange(n_pair, dtype=I32)
    _, order = lax.sort((e_flat, pair_ids), num_keys=1, is_stable=True)
    _, sorted_pos = lax.sort((order, pair_ids), num_keys=1)
    e_ids = jnp.arange(N_EXPERTS, dtype=I32)
    below = jnp.sum((e_flat[None, :] < e_ids[:, None]).astype(I32), axis=1)
    counts = jnp.concatenate([below[1:], jnp.full((1,), n_pair, I32)]) - below
    padded = ((counts + tm - 1) // tm) * tm
    ends = jnp.cumsum(padded)
    starts = ends - padded
    dst = (sorted_pos + (starts - below)[e_flat]).astype(I32)
    p_rows = 2 * n_tok + N_EXPERTS * tm
    n_tiles = p_rows // tm
    n_used = (ends[-1] // tm).astype(I32)
    tile_start = jnp.arange(n_tiles, dtype=I32) * tm
    tile_e = jnp.sum((ends[None, :] <= tile_start[:, None]).astype(I32), axis=1)
    tile_e = jnp.minimum(tile_e, N_EXPERTS - 1)
    active = jnp.arange(n_tiles, dtype=I32) < n_used
    shift = (starts - below)[tile_e]
    valid_rows = jnp.where(active, jnp.minimum(counts[tile_e] - (tile_start - starts[tile_e]), tm), 0)
    slot = jnp.arange(p_rows, dtype=I32).reshape(n_tiles, tm)
    pair = order[jnp.clip(slot - shift[:, None], 0, 2 * n_tok - 1)]
    in_tile = jnp.arange(tm, dtype=I32)[None, :] < valid_rows[:, None]
    tok = jnp.where(pair >= n_tok, pair - n_tok, pair)
    src_row = jnp.where(in_tile, tok, 0).astype(I32).reshape(-1)
    last_e = tile_e[jnp.maximum(n_used - 1, 0)]
    tile_e = jnp.where(active, tile_e, last_e)
    tile_first = (active & (tile_start == starts[tile_e])).astype(I32)
    tile_wslot = lax.rem(jnp.cumsum(tile_first) - 1, 2).astype(I32)
    e_ids = jnp.arange(N_EXPERTS, dtype=I32)
    later_used = (counts[None, :] > 0) & (e_ids[None, :] > e_ids[:, None])
    next_used = jnp.min(jnp.where(later_used, e_ids[None, :], N_EXPERTS), axis=1)
    next_used = jnp.where(next_used == N_EXPERTS, -1, next_used).astype(I32)
    tile_next = next_used[tile_e]
    return tile_e, tile_first, tile_next, tile_wslot, n_used.reshape(1), src_row, dst


def _forward(x0, x1, groups, params, *, tm_proj, tn_proj, rc_proj, tq, tk, tm_post, tm_moe, tm_comb):
    (attn_norm_g, w_in, q_norm_g, k_norm_g, lambda_q1, lambda_k1, lambda_q2, lambda_k2, subln_g,
     w_attn_up, w_pool_grp, pool_scale, w_pool_up, w_out, ffn_norm_g, w_group_router,
     b_group_router, w_expert_router, b_expert_router, w_gate, w_up, w_down) = params
    t0, t1 = x0.shape[0], x1.shape[0]
    n_tok = t0 + t1
    lam_init = 0.8 - 0.6 * math.exp(-0.3 * 0)
    max_len = max(s for _, s in groups)

    cos_t, sin_t = _rope_tables(max_len)
    pos_proj, _ = _tile_tables(groups, tm_proj)
    q_scale = HEAD_DIM ** -0.5 * math.log2(math.e)
    qkg = jnp.zeros((8, ATTN_W), F32)
    qkg = qkg.at[0].set(jnp.tile(q_norm_g, ATTN_W // HEAD_DIM) * q_scale)
    qkg = qkg.at[1].set(jnp.tile(k_norm_g, ATTN_W // HEAD_DIM))
    seg = np.arange(LANES) // HEAD_DIM
    bd = jnp.asarray((seg[:, None] == seg[None, :]).astype(np.float32) / HEAD_DIM, BF16)

    proj = _in_projection(x0, x1, jnp.asarray(pos_proj // tm_proj), attn_norm_g[None, :],
                          w_in.astype(BF16), cos_t, sin_t, qkg, bd, tm=tm_proj, tn=tn_proj, rc=rc_proj)

    lam_pack = jnp.zeros((8, HEAD_DIM), F32)
    lam_pack = lam_pack.at[0].set(lambda_q1).at[1].set(lambda_k1).at[2].set(lambda_q2).at[3].set(lambda_k2)
    outs = []
    tok_off = 0
    for batch, s_len in groups:
        outs.append(_attention(proj, tok_off, batch, s_len, lam_pack, subln_g[None, :],
                               tq=min(tq, s_len), tk=min(tk, s_len), lam_init=lam_init))
        tok_off += batch * s_len

    tile_pos, tile_len = _tile_tables(groups, tm_post)
    w_router = jnp.zeros((D_MODEL, ROUTER_COLS), F32)
    w_router = w_router.at[:, 0:N_GROUPS].set(w_group_router)
    w_router = w_router.at[:, EXPERT_COL0:EXPERT_COL0 + N_EXPERTS].set(w_expert_router)
    b_router = jnp.zeros((1, ROUTER_COLS), F32)
    b_router = b_router.at[0, 0:N_GROUPS].set(b_group_router)
    b_router = b_router.at[0, EXPERT_COL0:EXPERT_COL0 + N_EXPERTS].set(b_expert_router)
    wr_hi = w_router.astype(BF16)
    wr_lo = (w_router - wr_hi.astype(F32)).astype(BF16)
    xnew, ids, wts = _post_block(
        x0, x1, outs[0], outs[1], proj, jnp.asarray(tile_pos), jnp.asarray(tile_len),
        w_attn_up.astype(BF16), w_pool_grp.astype(BF16), pool_scale[None, :], w_pool_up.astype(BF16),
        w_out.astype(BF16), ffn_norm_g[None, :], jnp.concatenate([wr_hi, wr_lo], axis=1), b_router,
        tm=tm_post)

    tile_e, tile_first, tile_next, tile_wslot, n_used, src_row, dst = _routing_tables(ids, n_tok, tm_moe)
    y_sorted = _moe_ffn(
        xnew, tile_e, tile_first, tile_next, tile_wslot, n_used, src_row, ffn_norm_g[None, :],
        w_gate.reshape(N_EXPERTS, D_MODEL, D_EXPERT), w_up.reshape(N_EXPERTS, D_MODEL, D_EXPERT),
        w_down.reshape(N_EXPERTS, D_EXPERT, D_MODEL), tm=tm_moe)
    return _combine(xnew, y_sorted, wts, dst, t0, tm=tm_comb)


def kernel(x_prompt, x_sample, attn_norm_g, w_in, q_norm_g, k_norm_g, lambda_q1, lambda_k1, lambda_q2,
           lambda_k2, subln_g, w_attn_up, w_pool_grp, pool_scale, w_pool_up, w_out, ffn_norm_g,
           w_group_router, b_group_router, w_expert_router, b_expert_router, w_gate, w_up, w_down):
    assert attn_norm_g.shape[0] == 1, "single-layer stack"
    params = tuple(p[0] for p in (
        attn_norm_g, w_in, q_norm_g, k_norm_g, lambda_q1, lambda_k1, lambda_q2, lambda_k2, subln_g,
        w_attn_up, w_pool_grp, pool_scale, w_pool_up, w_out, ffn_norm_g, w_group_router,
        b_group_router, w_expert_router, b_expert_router, w_gate, w_up, w_down))
    groups = (x_prompt.shape[:2], x_sample.shape[:2])
    y0, y1 = _forward(
        x_prompt.reshape(-1, D_MODEL), x_sample.reshape(-1, D_MODEL), groups, params,
        tm_proj=1024, tn_proj=1024, rc_proj=512, tq=4096, tk=512, tm_post=256, tm_moe=256, tm_comb=256)
    return y0.reshape(x_prompt.shape), y1.reshape(x_sample.shape)
```

```python
import functools
import math

import numpy as np
import jax
import jax.numpy as jnp
from jax import lax
from jax.experimental import pallas as pl
from jax.experimental.pallas import tpu as pltpu

F32 = jnp.float32
BF16 = jnp.bfloat16
I32 = jnp.int32

D_MODEL = 2048
HEAD_DIM = 64
N_HEADS = 8
HEAD_W = 2 * HEAD_DIM
ATTN_W = N_HEADS * HEAD_W
POOL_W = 1024
POOL_WINDOWS = (2, 4, 8, 16)
POOL_GROUP_W = POOL_W // len(POOL_WINDOWS)
POOL_HALO = 64
IN_COLS = 3 * ATTN_W + POOL_W + 2 * D_MODEL
N_GROUPS = 4
EXPERTS_PER_GROUP = 8
N_EXPERTS = N_GROUPS * EXPERTS_PER_GROUP
D_EXPERT = 512
ROPE_THETA = 10000.0
EPS = 1e-6
LANES = 128
ROUTER_COLS = 128
EXPERT_COL0 = 8
ROW_TILES = D_MODEL // LANES
ROW_PITCH = ROW_TILES + 1
SAFE_SCORE_BOUND = 50.0
SCORE_BOUND_MARGIN = 1.02

COL_Q, COL_K, COL_V, COL_PIN, COL_GA, COL_GP = 0, 1, 2, 3, 4, 6

VMEM_LIMIT = 56 * 1024 * 1024


def _cparams(sem, vmem=VMEM_LIMIT):
    return pltpu.CompilerParams(dimension_semantics=sem, vmem_limit_bytes=vmem)


def _token_slab(ref, n_tok, s):
    return ref[pl.ds(s, n_tok, stride=ROW_PITCH), :]


def _load_token_rows(ref, n_tok):
    return jnp.concatenate([_token_slab(ref, n_tok, s) for s in range(ROW_TILES)], axis=1)


def _store_token_rows(ref, value):
    n_tok = value.shape[0]
    for s in range(ROW_TILES):
        ref[pl.ds(s, n_tok, stride=ROW_PITCH), :] = value[:, s * LANES:(s + 1) * LANES]
    ref[pl.ds(ROW_TILES, n_tok, stride=ROW_PITCH), :] = jnp.zeros((n_tok, LANES), ref.dtype)


def _inproj_kernel(posblk_ref, x0_hbm, x1_hbm, gattn_ref, w_ref, cos_ref, sin_ref, qkg_ref, bd_ref,
                   o_ref, xn_ref, xbuf, xsem, *, nt0, rc):
    del posblk_ref
    i = pl.program_id(0)
    j = pl.program_id(1)
    nt = pl.num_programs(0)
    tm = xn_ref.shape[0]
    n_chunks = tm // rc

    def x_tile_copy(x_hbm, tile):
        return pltpu.make_async_copy(x_hbm.at[pl.ds(tile * tm, tm)], xbuf, xsem)

    def start_x_tile(tile):
        @pl.when(tile < nt0)
        def _():
            x_tile_copy(x0_hbm, tile).start()

        @pl.when(tile >= nt0)
        def _():
            x_tile_copy(x1_hbm, tile - nt0).start()

    @pl.when(j == 0)
    def _():
        @pl.when(i == 0)
        def _():
            start_x_tile(i)

        x_tile_copy(x0_hbm, 0).wait()
        for c in range(n_chunks):
            xs = xbuf[c * rc:(c + 1) * rc, :]
            ms = jnp.mean(xs * xs, axis=-1, keepdims=True)
            xn_ref[c * rc:(c + 1) * rc, :] = (xs * lax.rsqrt(ms + EPS) * gattn_ref[...]).astype(BF16)

        @pl.when(i + 1 < nt)
        def _():
            start_x_tile(i + 1)

    def chunk_acc(c):
        return jnp.dot(xn_ref[c * rc:(c + 1) * rc, :], w_ref[...], preferred_element_type=F32)

    @pl.when(j <= COL_K)
    def _():
        g = jnp.where(j == COL_Q, qkg_ref[0:1, :], qkg_ref[1:2, :])
        lane = lax.broadcasted_iota(I32, (rc, LANES), 1)
        first_half = (lane % HEAD_DIM) < (HEAD_DIM // 2)
        for c in range(n_chunks):
            a = chunk_acc(c)
            cos = cos_ref[c * rc:(c + 1) * rc, :]
            sin = sin_ref[c * rc:(c + 1) * rc, :]
            for t in range(a.shape[1] // LANES):
                at = a[:, t * LANES:(t + 1) * LANES]
                ms = jnp.dot((at * at).astype(BF16), bd_ref[...], preferred_element_type=F32)
                u = at * g[:, t * LANES:(t + 1) * LANES]
                rot = jnp.where(first_half, pltpu.roll(u, LANES - HEAD_DIM // 2, 1),
                                pltpu.roll(u, HEAD_DIM // 2, 1))
                val = (u * cos + rot * sin) * lax.rsqrt(ms + EPS)
                o_ref[c * rc:(c + 1) * rc, t * LANES:(t + 1) * LANES] = val.astype(BF16)

    @pl.when((j == COL_V) | (j == COL_PIN))
    def _():
        for c in range(n_chunks):
            o_ref[c * rc:(c + 1) * rc, :] = chunk_acc(c).astype(BF16)

    @pl.when(j >= COL_GA)
    def _():
        for c in range(n_chunks):
            o_ref[c * rc:(c + 1) * rc, :] = (0.5 * jnp.tanh(0.5 * chunk_acc(c)) + 0.5).astype(BF16)


def _in_projection(x0, x1, posblk, gattn, w_in, cos_t, sin_t, qkg, bd, *, tm, tn, rc):
    t0, t1 = x0.shape[0], x1.shape[0]
    nt0, nt1 = t0 // tm, t1 // tm
    nt = nt0 + nt1
    grid_spec = pltpu.PrefetchScalarGridSpec(
        num_scalar_prefetch=1,
        grid=(nt, IN_COLS // tn),
        in_specs=[
            pl.BlockSpec(memory_space=pl.ANY),
            pl.BlockSpec(memory_space=pl.ANY),
            pl.BlockSpec((1, D_MODEL), lambda i, j, pb: (0, 0)),
            pl.BlockSpec((D_MODEL, tn), lambda i, j, pb: (0, j)),
            pl.BlockSpec((tm, LANES), lambda i, j, pb: (pb[i], 0)),
            pl.BlockSpec((tm, LANES), lambda i, j, pb: (pb[i], 0)),
            pl.BlockSpec((8, tn), lambda i, j, pb: (0, 0)),
            pl.BlockSpec((LANES, LANES), lambda i, j, pb: (0, 0)),
        ],
        out_specs=pl.BlockSpec((tm, tn), lambda i, j, pb: (i, j)),
        scratch_shapes=[
            pltpu.VMEM((tm, D_MODEL), BF16),
            pltpu.VMEM((tm, D_MODEL), F32),
            pltpu.SemaphoreType.DMA(()),
        ],
    )
    return pl.pallas_call(
        functools.partial(_inproj_kernel, nt0=nt0, rc=rc),
        grid_spec=grid_spec,
        out_shape=jax.ShapeDtypeStruct((t0 + t1, IN_COLS), BF16),
        compiler_params=_cparams(("arbitrary", "arbitrary")),
        name="in_projection",
    )(posblk, x0, x1, gattn, w_in, cos_t, sin_t, qkg, bd)


def _attn_kernel(q_ref, k_ref, v_ref, lam_ref, sg_ref, o_ref, vt_ref, acc_ref, m_ref, l_ref, kn_ref,
                 p_ref, *, tk, lam_init):
    i = pl.program_id(2)
    s_len = k_ref.shape[0]
    tq = q_ref.shape[0]
    nk = s_len // tk
    nt_dims = (((1,), (1,)), ((), ()))

    sel_r = lax.broadcasted_iota(I32, (8, HEAD_W), 0)
    sel_l = lax.broadcasted_iota(I32, (8, HEAD_W), 1)
    sel = jnp.where(sel_l // HEAD_DIM == sel_r, 1.0, 0.0).astype(BF16)

    @pl.when(i == 0)
    def _():
        kn2 = jnp.zeros((8, 1), F32)
        for jb in range(nk):
            rows = slice(jb * tk, (jb + 1) * tk)
            vt_ref[:, rows] = v_ref[rows, :].astype(F32).T.astype(BF16)
            kf = k_ref[rows, :].astype(F32)
            n2 = lax.dot_general(sel, (kf * kf).astype(BF16), nt_dims, preferred_element_type=F32)
            kn2 = jnp.maximum(kn2, jnp.max(n2, axis=1, keepdims=True))
        kn_ref[...] = jnp.broadcast_to(kn2, kn_ref.shape)

    q = q_ref[...]
    lane = lax.broadcasted_iota(I32, q.shape, 1)
    zero = jnp.zeros_like(q)
    qc = (jnp.where(lane < HEAD_DIM, q, zero), jnp.where(lane >= HEAD_DIM, q, zero))

    qf = q.astype(F32)
    qn2 = lax.dot_general(sel, (qf * qf).astype(BF16), nt_dims, preferred_element_type=F32)
    bound = jnp.sqrt(qn2 * kn_ref[:, 0:1]) * SCORE_BOUND_MARGIN
    bound_c = (bound[0:1, :], bound[1:2, :])
    use_bound = jnp.max(bound[0:2, :]) <= SAFE_SCORE_BOUND

    l_ref[...] = jnp.zeros(l_ref.shape, F32)
    acc_ref[...] = jnp.zeros(acc_ref.shape, F32)

    def load_blocks(jb):
        off = jb * tk if isinstance(jb, int) else pl.multiple_of(jb * tk, tk)
        return k_ref[pl.ds(off, tk), :], vt_ref[:, pl.ds(off, tk)]

    def bounded_weights(jb, slot):
        kb, _ = load_blocks(jb)
        for c in range(2):
            st = lax.dot_general(kb, qc[c], nt_dims, preferred_element_type=F32)
            p = jnp.exp2(st - bound_c[c])
            l_ref[c] += jnp.sum(p, axis=0, keepdims=True)
            p_ref[slot, c] = p.astype(BF16)

    def weighted_values(jb, slot):
        _, vtb = load_blocks(jb)
        for c in range(2):
            acc_ref[c] += jnp.dot(vtb, p_ref[slot, c], preferred_element_type=F32)

    def bounded_pair(m, carry):
        j0 = 2 * m
        bounded_weights(j0 + 1, 1)
        weighted_values(j0, 0)
        bounded_weights(j0 + 2, 0)
        weighted_values(j0 + 1, 1)
        return carry

    def running_max_step(jb, carry):
        kb, vtb = load_blocks(jb)
        for c in range(2):
            st = lax.dot_general(kb, qc[c], nt_dims, preferred_element_type=F32)
            m_old = m_ref[c]
            m_new = jnp.maximum(m_old, jnp.max(st, axis=0, keepdims=True))
            alpha = jnp.exp2(m_old - m_new)
            p = jnp.exp2(st - m_new)
            l_ref[c] = alpha * l_ref[c] + jnp.sum(p, axis=0, keepdims=True)
            acc_ref[c] = alpha * acc_ref[c] + jnp.dot(vtb, p.astype(BF16), preferred_element_type=F32)
            m_ref[c] = m_new
        return carry

    @pl.when(use_bound)
    def _():
        bounded_weights(0, 0)
        lax.fori_loop(0, nk // 2 - 1, bounded_pair, 0)
        bounded_weights(nk - 1, 1)
        weighted_values(nk - 2, 0)
        weighted_values(nk - 1, 1)

    @pl.when(jnp.logical_not(use_bound))
    def _():
        m_ref[...] = jnp.full(m_ref.shape, -jnp.inf, F32)
        lax.fori_loop(0, nk, running_max_step, 0)

    lq1, lk1, lq2, lk2 = lam_ref[0:1, :], lam_ref[1:2, :], lam_ref[2:3, :], lam_ref[3:4, :]
    lam = (jnp.exp(jnp.sum(lq1 * lk1, axis=-1, keepdims=True))
           - jnp.exp(jnp.sum(lq2 * lk2, axis=-1, keepdims=True)) + lam_init)
    ot = acc_ref[0] / l_ref[0] - lam * (acc_ref[1] / l_ref[1])
    ms = jnp.mean(ot * ot, axis=0, keepdims=True)
    on = (ot * lax.rsqrt(ms + EPS)).T
    o_ref[...] = (on * sg_ref[...] * (1.0 - lam_init)).astype(BF16)


def _attention(proj, tok_off, batch, s_len, lam_pack, subln_g, *, tq, tk, lam_init):
    qb0 = tok_off // tq
    sb0 = tok_off // s_len
    nq = s_len // tq
    return pl.pallas_call(
        functools.partial(_attn_kernel, tk=tk, lam_init=lam_init),
        grid=(batch, N_HEADS, nq),
        in_specs=[
            pl.BlockSpec((tq, HEAD_W), lambda b, h, i: (qb0 + b * nq + i, COL_Q * N_HEADS + h)),
            pl.BlockSpec((s_len, HEAD_W), lambda b, h, i: (sb0 + b, COL_K * N_HEADS + h)),
            pl.BlockSpec((s_len, HEAD_W), lambda b, h, i: (sb0 + b, COL_V * N_HEADS + h)),
            pl.BlockSpec((8, HEAD_DIM), lambda b, h, i: (0, 0)),
            pl.BlockSpec((1, HEAD_W), lambda b, h, i: (0, 0)),
        ],
        out_specs=pl.BlockSpec((tq, HEAD_W), lambda b, h, i: (b * nq + i, h)),
        out_shape=jax.ShapeDtypeStruct((batch * s_len, ATTN_W), BF16),
        scratch_shapes=[
            pltpu.VMEM((HEAD_W, s_len), BF16),
            pltpu.VMEM((2, HEAD_W, tq), F32),
            pltpu.VMEM((2, 1, tq), F32),
            pltpu.VMEM((2, 1, tq), F32),
            pltpu.VMEM((8, LANES), F32),
            pltpu.VMEM((2, 2, tk, tq), BF16),
        ],
        compiler_params=_cparams(("arbitrary", "arbitrary", "arbitrary")),
        name="diff_attention",
    )(proj, proj, proj, lam_pack, subln_g)


def _post_kernel(pos_ref, len_ref, x0_ref, x1_ref, o0_ref, o1_ref, pin_ref, prev_ref, next_ref,
                 ga_ref, gp_ref, wau_ref, wgrp_ref, pscale_ref, wpu_ref, wout_ref, gffn_ref,
                 wrc_ref, br_ref, xnew_ref, ids_ref, wts_ref, *, nt0):
    i = pl.program_id(0)
    tm = pin_ref.shape[0]
    p0 = pos_ref[i]
    s_len = len_ref[i]
    first = i < nt0

    o = jnp.where(first, o0_ref[...], o1_ref[...])
    attn_d = jnp.dot(o, wau_ref[...], preferred_element_type=F32)

    ext = jnp.concatenate([prev_ref[...], pin_ref[...], next_ref[...]], axis=0)
    ke = tm + 2 * POOL_HALO
    r_band = lax.broadcasted_iota(I32, (tm, ke), 0) + p0
    c_band = lax.broadcasted_iota(I32, (tm, ke), 1) + (p0 - POOL_HALO)
    r_cnt = lax.broadcasted_iota(I32, (tm, POOL_GROUP_W), 0) + p0
    pooled = []
    for gi, w in enumerate(POOL_WINDOWS):
        back, fwd = w // 2, w - 1 - w // 2
        lo = jnp.maximum(r_band - back, 0)
        hi = jnp.minimum(r_band + fwd, s_len - 1)
        band = jnp.where((c_band >= lo) & (c_band <= hi), 1.0, 0.0).astype(BF16)
        cnt = jnp.minimum(r_cnt + fwd, s_len - 1) - jnp.maximum(r_cnt - back, 0) + 1
        cols = slice(gi * POOL_GROUP_W, (gi + 1) * POOL_GROUP_W)
        wsum = jnp.dot(band, ext[:, cols], preferred_element_type=F32)
        y = wsum / cnt.astype(F32) - pin_ref[:, cols].astype(F32)
        pooled.append(jnp.dot(y.astype(BF16), wgrp_ref[gi], preferred_element_type=F32))
    pooled = (jnp.concatenate(pooled, axis=1) * pscale_ref[...]).astype(BF16)
    pool_d = jnp.dot(pooled, wpu_ref[...], preferred_element_type=F32)

    merged = ga_ref[...].astype(F32) * attn_d + gp_ref[...].astype(F32) * pool_d
    x = jnp.where(first, x0_ref[...], x1_ref[...])
    xnew = x + jnp.dot(merged.astype(BF16), wout_ref[...], preferred_element_type=F32)
    _store_token_rows(xnew_ref, xnew)

    ms = jnp.mean(xnew * xnew, axis=-1, keepdims=True)
    hn = xnew * lax.rsqrt(ms + EPS) * gffn_ref[...]
    hi_part = hn.astype(BF16)
    lo_part = (hn - hi_part.astype(F32)).astype(BF16)
    hw = jnp.dot(hi_part, wrc_ref[...], preferred_element_type=F32)
    logits = (hw[:, 0:ROUTER_COLS]
              + (hw[:, ROUTER_COLS:]
                 + jnp.dot(lo_part, wrc_ref[:, 0:ROUTER_COLS], preferred_element_type=F32))
              + br_ref[...])
    lt = logits.T

    gl = lt[0:N_GROUPS, :]
    g_iota = lax.broadcasted_iota(I32, gl.shape, 0)
    g_max = jnp.max(gl, axis=0, keepdims=True)
    g_sel = jnp.min(jnp.where(gl == g_max, g_iota, N_GROUPS), axis=0, keepdims=True)
    g_w = 1.0 / jnp.sum(jnp.exp(gl - g_max), axis=0, keepdims=True)
    el = lt[EXPERT_COL0:EXPERT_COL0 + N_EXPERTS, :]
    e_iota = lax.broadcasted_iota(I32, el.shape, 0)
    neg = jnp.float32(-jnp.inf)
    el = jnp.where(e_iota // EXPERTS_PER_GROUP == g_sel, el, neg)
    v1 = jnp.max(el, axis=0, keepdims=True)
    i1 = jnp.min(jnp.where(el == v1, e_iota, N_EXPERTS), axis=0, keepdims=True)
    el2 = jnp.where(e_iota == i1, neg, el)
    v2 = jnp.max(el2, axis=0, keepdims=True)
    i2 = jnp.min(jnp.where(el2 == v2, e_iota, N_EXPERTS), axis=0, keepdims=True)
    d = jnp.exp(v2 - v1)
    w1 = g_w / (1.0 + d)
    w2 = g_w * d / (1.0 + d)
    row = lax.broadcasted_iota(I32, (8, tm), 0)
    ids_ref[...] = jnp.where(row == 0, i1, jnp.where(row == 1, i2, 0))
    wts_ref[...] = jnp.where(row == 0, w1, jnp.where(row == 1, w2, 0.0))


def _post_block(x0, x1, o0, o1, proj, tile_pos, tile_len, wau, wgrp, pscale, wpu, wout, gffn,
                wrc, br, *, tm):
    t0, t1 = x0.shape[0], x1.shape[0]
    t = t0 + t1
    nt0 = t0 // tm
    nt = t // tm
    hb = tm // POOL_HALO
    n_hblk = t // POOL_HALO
    const = lambda *shape: pl.BlockSpec(shape, lambda i, tp, tl: (0,) * len(shape),
                                        pipeline_mode=pl.Buffered(1))
    grid_spec = pltpu.PrefetchScalarGridSpec(
        num_scalar_prefetch=2,
        grid=(nt,),
        in_specs=[
            pl.BlockSpec((tm, D_MODEL), lambda i, tp, tl: (jnp.minimum(i, nt0 - 1), 0)),
            pl.BlockSpec((tm, D_MODEL), lambda i, tp, tl: (jnp.maximum(i - nt0, 0), 0)),
            pl.BlockSpec((tm, ATTN_W), lambda i, tp, tl: (jnp.minimum(i, nt0 - 1), 0)),
            pl.BlockSpec((tm, ATTN_W), lambda i, tp, tl: (jnp.maximum(i - nt0, 0), 0)),
            pl.BlockSpec((tm, POOL_W), lambda i, tp, tl: (i, COL_PIN)),
            pl.BlockSpec((POOL_HALO, POOL_W), lambda i, tp, tl: (jnp.maximum(i * hb - 1, 0), COL_PIN)),
            pl.BlockSpec((POOL_HALO, POOL_W), lambda i, tp, tl: (jnp.minimum((i + 1) * hb, n_hblk - 1), COL_PIN)),
            pl.BlockSpec((tm, D_MODEL), lambda i, tp, tl: (i, COL_GA // 2)),
            pl.BlockSpec((tm, D_MODEL), lambda i, tp, tl: (i, COL_GP // 2)),
            const(ATTN_W, D_MODEL),
            const(len(POOL_WINDOWS), POOL_GROUP_W, POOL_GROUP_W),
            const(1, POOL_W),
            const(POOL_W, D_MODEL),
            const(D_MODEL, D_MODEL),
            const(1, D_MODEL),
            const(D_MODEL, 2 * ROUTER_COLS),
            const(1, ROUTER_COLS),
        ],
        out_specs=[
            pl.BlockSpec((tm * ROW_PITCH, LANES), lambda i, tp, tl: (i, 0)),
            pl.BlockSpec((8, tm), lambda i, tp, tl: (0, i)),
            pl.BlockSpec((8, tm), lambda i, tp, tl: (0, i)),
        ],
    )
    return pl.pallas_call(
        functools.partial(_post_kernel, nt0=nt0),
        grid_spec=grid_spec,
        out_shape=[
            jax.ShapeDtypeStruct((t * ROW_PITCH, LANES), F32),
            jax.ShapeDtypeStruct((8, t), I32),
            jax.ShapeDtypeStruct((8, t), F32),
        ],
        compiler_params=_cparams(("arbitrary",)),
        name="post_block",
    )(tile_pos, tile_len, x0, x1, o0, o1, proj, proj, proj, proj, proj, wau, wgrp, pscale, wpu,
      wout, gffn, wrc, br)


GATHER_UNROLL = 8
WEIGHT_DMA_PRIORITY = 1


def _row_gather_start(src_hbm, idx_ref, base, n_rows, buf, sem, priority=0):
    def body(r, carry):
        src0 = idx_ref[base + r] * ROW_PITCH
        dst0 = r * ROW_PITCH
        pltpu.make_async_copy(src_hbm.at[pl.ds(src0, ROW_TILES)], buf.at[pl.ds(dst0, ROW_TILES)],
                              sem).start(priority=priority)
        return carry
    lax.fori_loop(0, n_rows, body, 0, unroll=GATHER_UNROLL)


def _row_gather_wait(src_hbm, n_rows, buf, sem):
    n = n_rows * ROW_TILES
    pltpu.make_async_copy(src_hbm.at[pl.ds(0, n)], buf.at[pl.ds(0, n)], sem).wait()


def _moe_kernel(te_ref, tf_ref, tn_ref, ts_ref, nu_ref, src_ref, x_hbm, gffn_ref, wg_hbm, wu_hbm, wd_hbm,
                y_ref, xbuf, sem, wgf, wuf, wdf, wsem, wgb, wub, wdb):
    i = pl.program_id(0)
    tm = y_ref.shape[0] // ROW_PITCH
    n_used = nu_ref[0]
    slot = lax.rem(i, 2)

    def weight_copies(e, s):
        return (pltpu.make_async_copy(wg_hbm.at[e], wgf.at[s], wsem.at[s]),
                pltpu.make_async_copy(wu_hbm.at[e], wuf.at[s], wsem.at[s]),
                pltpu.make_async_copy(wd_hbm.at[e], wdf.at[s], wsem.at[s]))

    @pl.when(i == 0)
    def _():
        _row_gather_start(x_hbm, src_ref, 0, tm, xbuf.at[0], sem.at[0])
        for cp in weight_copies(te_ref[0], 0):
            cp.start(priority=WEIGHT_DMA_PRIORITY)

    @pl.when(i + 1 < n_used)
    def _():
        _row_gather_start(x_hbm, src_ref, (i + 1) * tm, tm, xbuf.at[1 - slot], sem.at[1 - slot])

    @pl.when(i < n_used)
    def _():
        @pl.when(tf_ref[i] == 1)
        def _():
            ws = ts_ref[i]
            for cp in weight_copies(te_ref[i], ws):
                cp.wait()
            for s in range(2):
                @pl.when(ws == s)
                def _(s=s):
                    wgb[...] = wgf[s].astype(BF16)
                    wub[...] = wuf[s].astype(BF16)
                    wdb[...] = wdf[s].astype(BF16)

            @pl.when(tn_ref[i] >= 0)
            def _():
                for cp in weight_copies(tn_ref[i], 1 - ws):
                    cp.start(priority=WEIGHT_DMA_PRIORITY)

        _row_gather_wait(x_hbm, tm, xbuf.at[slot], sem.at[slot])
        xs = _load_token_rows(xbuf.at[slot], tm)
        ms = jnp.mean(xs * xs, axis=-1, keepdims=True)
        hn = (xs * lax.rsqrt(ms + EPS) * gffn_ref[...]).astype(BF16)
        gate = jnp.dot(hn, wgb[...], preferred_element_type=F32)
        up = jnp.dot(hn, wub[...], preferred_element_type=F32)
        h = jax.nn.silu(gate) * up
        _store_token_rows(y_ref, jnp.dot(h.astype(BF16), wdb[...], preferred_element_type=F32))

    @pl.when(i >= n_used)
    def _():
        y_ref[...] = jnp.zeros(y_ref.shape, y_ref.dtype)


def _moe_ffn(xnew, tile_e, tile_first, tile_next, tile_wslot, n_used, src_row, gffn, w_gate, w_up, w_down,
             *, tm):
    p_rows = src_row.shape[0]
    n_tiles = p_rows // tm
    grid_spec = pltpu.PrefetchScalarGridSpec(
        num_scalar_prefetch=6,
        grid=(n_tiles,),
        in_specs=[
            pl.BlockSpec(memory_space=pl.ANY),
            pl.BlockSpec((1, D_MODEL), lambda i, *_: (0, 0)),
            pl.BlockSpec(memory_space=pl.ANY),
            pl.BlockSpec(memory_space=pl.ANY),
            pl.BlockSpec(memory_space=pl.ANY),
        ],
        out_specs=pl.BlockSpec((tm * ROW_PITCH, LANES), lambda i, *_: (i, 0)),
        scratch_shapes=[
            pltpu.VMEM((2, tm * ROW_PITCH, LANES), F32),
            pltpu.SemaphoreType.DMA((2,)),
            pltpu.VMEM((2, D_MODEL, D_EXPERT), F32),
            pltpu.VMEM((2, D_MODEL, D_EXPERT), F32),
            pltpu.VMEM((2, D_EXPERT, D_MODEL), F32),
            pltpu.SemaphoreType.DMA((2,)),
            pltpu.VMEM((D_MODEL, D_EXPERT), BF16),
            pltpu.VMEM((D_MODEL, D_EXPERT), BF16),
            pltpu.VMEM((D_EXPERT, D_MODEL), BF16),
        ],
    )
    return pl.pallas_call(
        _moe_kernel,
        grid_spec=grid_spec,
        out_shape=jax.ShapeDtypeStruct((p_rows * ROW_PITCH, LANES), F32),
        compiler_params=_cparams(("arbitrary",)),
        name="moe_ffn",
    )(tile_e, tile_first, tile_next, tile_wslot, n_used, src_row, xnew, gffn, w_gate, w_up, w_down)


def _combine_kernel(pos_ref, xnew_ref, wts_ref, y_hbm, out0_ref, out1_ref, ybuf, sem, *, nt0, n_tok):
    i = pl.program_id(0)
    nt = pl.num_programs(0)
    tm = xnew_ref.shape[0] // ROW_PITCH
    slot = lax.rem(i, 2)

    def start(tile, s):
        _row_gather_start(y_hbm, pos_ref, tile * tm, tm, ybuf.at[s, 0], sem.at[s])
        _row_gather_start(y_hbm, pos_ref, n_tok + tile * tm, tm, ybuf.at[s, 1], sem.at[s], priority=1)

    @pl.when(i == 0)
    def _():
        start(0, 0)

    @pl.when(i + 1 < nt)
    def _():
        start(i + 1, 1 - slot)

    _row_gather_wait(y_hbm, tm, ybuf.at[slot, 0], sem.at[slot])
    _row_gather_wait(y_hbm, tm, ybuf.at[slot, 1], sem.at[slot])
    wt = wts_ref[...].T
    w1 = jnp.broadcast_to(wt[:, 0:1], (tm, LANES))
    w2 = jnp.broadcast_to(wt[:, 1:2], (tm, LANES))

    def write(out_ref):
        for s in range(ROW_TILES):
            out_ref[:, s * LANES:(s + 1) * LANES] = (
                _token_slab(xnew_ref, tm, s)
                + (w1 * _token_slab(ybuf.at[slot, 0], tm, s) + w2 * _token_slab(ybuf.at[slot, 1], tm, s)))

    @pl.when(i < nt0)
    def _():
        write(out0_ref)

    @pl.when(i >= nt0)
    def _():
        write(out1_ref)


def _combine(xnew, y_sorted, wts, pos, t0, *, tm):
    t = xnew.shape[0] // ROW_PITCH
    nt0 = t0 // tm
    nt = t // tm
    grid_spec = pltpu.PrefetchScalarGridSpec(
        num_scalar_prefetch=1,
        grid=(nt,),
        in_specs=[
            pl.BlockSpec((tm * ROW_PITCH, LANES), lambda i, ps: (i, 0)),
            pl.BlockSpec((8, tm), lambda i, ps: (0, i)),
            pl.BlockSpec(memory_space=pl.ANY),
        ],
        out_specs=[
            pl.BlockSpec((tm, D_MODEL), lambda i, ps: (jnp.minimum(i, nt0 - 1), 0)),
            pl.BlockSpec((tm, D_MODEL), lambda i, ps: (jnp.maximum(i - nt0, 0), 0)),
        ],
        scratch_shapes=[
            pltpu.VMEM((2, 2, tm * ROW_PITCH, LANES), F32),
            pltpu.SemaphoreType.DMA((2,)),
        ],
    )
    return pl.pallas_call(
        functools.partial(_combine_kernel, nt0=nt0, n_tok=t),
        grid_spec=grid_spec,
        out_shape=[
            jax.ShapeDtypeStruct((t0, D_MODEL), F32),
            jax.ShapeDtypeStruct((t - t0, D_MODEL), F32),
        ],
        compiler_params=_cparams(("arbitrary",)),
        name="moe_combine",
    )(pos, xnew, wts, y_sorted)


def _rope_tables(max_len):
    inv = 1.0 / (ROPE_THETA ** (jnp.arange(0, HEAD_DIM, 2, dtype=F32) / HEAD_DIM))
    ang = jnp.arange(max_len, dtype=F32)[:, None] * inv[None, :]
    ang = jnp.concatenate([ang, ang, ang, ang], axis=-1)
    lane = jnp.arange(LANES)
    sign = jnp.where((lane % HEAD_DIM) < HEAD_DIM // 2, -1.0, 1.0).astype(F32)
    return jnp.cos(ang), jnp.sin(ang) * sign[None, :]


def _tile_tables(groups, tm):
    pos, length = [], []
    for batch, s_len in groups:
        for _ in range(batch):
            for p in range(0, s_len, tm):
                pos.append(p)
                length.append(s_len)
    return np.asarray(pos, np.int32), np.asarray(length, np.int32)


def _routing_tables(ids, n_tok, tm):
    n_pair = 2 * n_tok
    e_flat = ids[0:2, :].reshape(-1)
    pair_ids = jnp.arange(n_pair, dtype=I32)
    _, order = lax.sort((e_flat, pair_ids), num_keys=1, is_stable=True)
    _, sorted_pos = lax.sort((order, pair_ids), num_keys=1)
    e_ids = jnp.arange(N_EXPERTS, dtype=I32)
    below = jnp.sum((e_flat[None, :] < e_ids[:, None]).astype(I32), axis=1)
    counts = jnp.concatenate([below[1:], jnp.full((1,), n_pair, I32)]) - below
    padded = ((counts + tm - 1) // tm) * tm
    ends = jnp.cumsum(padded)
    starts = ends - padded
    dst = (sorted_pos + (starts - below)[e_flat]).astype(I32)
    p_rows = 2 * n_tok + N_EXPERTS * tm
    n_tiles = p_rows // tm
    n_used = (ends[-1] // tm).astype(I32)
    tile_start = jnp.arange(n_tiles, dtype=I32) * tm
    tile_e = jnp.sum((ends[None, :] <= tile_start[:, None]).astype(I32), axis=1)
    tile_e = jnp.minimum(tile_e, N_EXPERTS - 1)
    active = jnp.arange(n_tiles, dtype=I32) < n_used
    shift = (starts - below)[tile_e]
    valid_rows = jnp.where(active, jnp.minimum(counts[tile_e] - (tile_start - starts[tile_e]), tm), 0)
    slot = jnp.arange(p_rows, dtype=I32).reshape(n_tiles, tm)
    pair = order[jnp.clip(slot - shift[:, None], 0, 2 * n_tok - 1)]
    in_tile = jnp.arange(tm, dtype=I32)[None, :] < valid_rows[:, None]
    tok = jnp.where(pair >= n_tok, pair - n_tok, pair)
    src_row = jnp.where(in_tile, tok, 0).astype(I32).reshape(-1)
    last_e = tile_e[jnp.maximum(n_used - 1, 0)]
    tile_e = jnp.where(active, tile_e, last_e)
    tile_first = (active & (tile_start == starts[tile_e])).astype(I32)
    tile_wslot = lax.rem(jnp.cumsum(tile_first) - 1, 2).astype(I32)
    e_ids = jnp.arange(N_EXPERTS, dtype=I32)
    later_used = (counts[None, :] > 0) & (e_ids[None, :] > e_ids[:, None])
    next_used = jnp.min(jnp.where(later_used, e_ids[None, :], N_EXPERTS), axis=1)
    next_used = jnp.where(next_used == N_EXPERTS, -1, next_used).astype(I32)
    tile_next = next_used[tile_e]
    return tile_e, tile_first, tile_next, tile_wslot, n_used.reshape(1), src_row, dst


def _forward(x0, x1, groups, params, *, tm_proj, tn_proj, rc_proj, tq, tk, tm_post, tm_moe, tm_comb):
    (attn_norm_g, w_in, q_norm_g, k_norm_g, lambda_q1, lambda_k1, lambda_q2, lambda_k2, subln_g,
     w_attn_up, w_pool_grp, pool_scale, w_pool_up, w_out, ffn_norm_g, w_group_router,
     b_group_router, w_expert_router, b_expert_router, w_gate, w_up, w_down) = params
    t0, t1 = x0.shape[0], x1.shape[0]
    n_tok = t0 + t1
    lam_init = 0.8 - 0.6 * math.exp(-0.3 * 0)
    max_len = max(s for _, s in groups)

    cos_t, sin_t = _rope_tables(max_len)
    pos_proj, _ = _tile_tables(groups, tm_proj)
    q_scale = HEAD_DIM ** -0.5 * math.log2(math.e)
    qkg = jnp.zeros((8, ATTN_W), F32)
    qkg = qkg.at[0].set(jnp.tile(q_norm_g, ATTN_W // HEAD_DIM) * q_scale)
    qkg = qkg.at[1].set(jnp.tile(k_norm_g, ATTN_W // HEAD_DIM))
    seg = np.arange(LANES) // HEAD_DIM
    bd = jnp.asarray((seg[:, None] == seg[None, :]).astype(np.float32) / HEAD_DIM, BF16)

    proj = _in_projection(x0, x1, jnp.asarray(pos_proj // tm_proj), attn_norm_g[None, :],
                          w_in.astype(BF16), cos_t, sin_t, qkg, bd, tm=tm_proj, tn=tn_proj, rc=rc_proj)

    lam_pack = jnp.zeros((8, HEAD_DIM), F32)
    lam_pack = lam_pack.at[0].set(lambda_q1).at[1].set(lambda_k1).at[2].set(lambda_q2).at[3].set(lambda_k2)
    outs = []
    tok_off = 0
    for batch, s_len in groups:
        outs.append(_attention(proj, tok_off, batch, s_len, lam_pack, subln_g[None, :],
                               tq=min(tq, s_len), tk=min(tk, s_len), lam_init=lam_init))
        tok_off += batch * s_len

    tile_pos, tile_len = _tile_tables(groups, tm_post)
    w_router = jnp.zeros((D_MODEL, ROUTER_COLS), F32)
    w_router = w_router.at[:, 0:N_GROUPS].set(w_group_router)
    w_router = w_router.at[:, EXPERT_COL0:EXPERT_COL0 + N_EXPERTS].set(w_expert_router)
    b_router = jnp.zeros((1, ROUTER_COLS), F32)
    b_router = b_router.at[0, 0:N_GROUPS].set(b_group_router)
    b_router = b_router.at[0, EXPERT_COL0:EXPERT_COL0 + N_EXPERTS].set(b_expert_router)
    wr_hi = w_router.astype(BF16)
    wr_lo = (w_router - wr_hi.astype(F32)).astype(BF16)
    xnew, ids, wts = _post_block(
        x0, x1, outs[0], outs[1], proj, jnp.asarray(tile_pos), jnp.asarray(tile_len),
        w_attn_up.astype(BF16), w_pool_grp.astype(BF16), pool_scale[None, :], w_pool_up.astype(BF16),
        w_out.astype(BF16), ffn_norm_g[None, :], jnp.concatenate([wr_hi, wr_lo], axis=1), b_router,
        tm=tm_post)

    tile_e, tile_first, tile_next, tile_wslot, n_used, src_row, dst = _routing_tables(ids, n_tok, tm_moe)
    y_sorted = _moe_ffn(
        xnew, tile_e, tile_first, tile_next, tile_wslot, n_used, src_row, ffn_norm_g[None, :],
        w_gate.reshape(N_EXPERTS, D_MODEL, D_EXPERT), w_up.reshape(N_EXPERTS, D_MODEL, D_EXPERT),
        w_down.reshape(N_EXPERTS, D_EXPERT, D_MODEL), tm=tm_moe)
    return _combine(xnew, y_sorted, wts, dst, t0, tm=tm_comb)


def kernel(x_prompt, x_sample, attn_norm_g, w_in, q_norm_g, k_norm_g, lambda_q1, lambda_k1, lambda_q2,
           lambda_k2, subln_g, w_attn_up, w_pool_grp, pool_scale, w_pool_up, w_out, ffn_norm_g,
           w_group_router, b_group_router, w_expert_router, b_expert_router, w_gate, w_up, w_down):
    assert attn_norm_g.shape[0] == 1, "single-layer stack"
    params = tuple(p[0] for p in (
        attn_norm_g, w_in, q_norm_g, k_norm_g, lambda_q1, lambda_k1, lambda_q2, lambda_k2, subln_g,
        w_attn_up, w_pool_grp, pool_scale, w_pool_up, w_out, ffn_norm_g, w_group_router,
        b_group_router, w_expert_router, b_expert_router, w_gate, w_up, w_down))
    groups = (x_prompt.shape[:2], x_sample.shape[:2])
    y0, y1 = _forward(
        x_prompt.reshape(-1, D_MODEL), x_sample.reshape(-1, D_MODEL), groups, params,
        tm_proj=2048, tn_proj=1024, rc_proj=512, tq=4096, tk=512, tm_post=256, tm_moe=256, tm_comb=256)
    return y0.reshape(x_prompt.shape), y1.reshape(x_sample.shape)
```
